```python
import math
import jax, jax.numpy as jnp
from jax import lax
import numpy as np

D_MODEL = 4096
BATCH = 4
SEQ = 4096
DEPTH = 4

CHUNK = 128
EPS = 1e-6
GATE_CAP = 15.0
ML_HEADS = 4
ML_DV = D_MODEL // (2 * ML_HEADS)
ML_DK = ML_DV // 2
ML_QK = ML_HEADS * ML_DK
ML_V = ML_HEADS * ML_DV
GDN_DK = 128
GDN_DV = 128
GDN_HEADS = D_MODEL // (2 * GDN_DV)
GDN_QK = GDN_HEADS * GDN_DK
GDN_V = GDN_HEADS * GDN_DV
GDN_CONV = 4
MIX_WIDTH = ML_V + GDN_V
EVEN_SPLITS = (ML_QK, ML_QK, ML_V, ML_V, ML_HEADS, ML_HEADS,
               2 * GDN_QK + GDN_V, GDN_V, GDN_HEADS, GDN_HEADS)
EVEN_IN = sum(EVEN_SPLITS)
M2_DINNER = 2 * D_MODEL
M2_HEADDIM = 64
M2_HEADS = M2_DINNER // M2_HEADDIM
M2_GROUPS = 8
M2_DSTATE = 128
M2_CONV = 4
M2_XBC = M2_DINNER + 2 * M2_GROUPS * M2_DSTATE
M2_SPLITS = (M2_DINNER, M2_XBC, M2_HEADS)
M2_IN = sum(M2_SPLITS)
FFN_DIM = 2 * D_MODEL
FFN_CONV = 3
N_EVEN = (DEPTH + 1) // 2
N_ODD = DEPTH // 2

kernel_name = "hybrid_mlstm_gdn_mamba2_convffn"


def split_cols(t, sizes):
    idx = [int(i) for i in np.cumsum(sizes)[:-1]]
    return jnp.split(t, idx, axis=-1)


def rms_normalize(t):
    tf = t.astype(jnp.float32)
    return tf * lax.rsqrt(jnp.mean(tf * tf, axis=-1, keepdims=True) + EPS)


def rmsnorm(t, g):
    return (rms_normalize(t) * g).astype(t.dtype)


def l2norm(t):
    return t * lax.rsqrt(jnp.sum(t * t, axis=-1, keepdims=True) + EPS)


def soft_cap(t):
    return GATE_CAP * jnp.tanh(t / GATE_CAP)


def causal_dwconv(t, w, b=None):
    K, C = w.shape
    y = lax.conv_general_dilated(t, w.reshape(K, 1, C).astype(t.dtype), window_strides=(1,),
                                 padding=[(K - 1, 0)], dimension_numbers=('NWC', 'WIO', 'NWC'),
                                 feature_group_count=C)
    return y if b is None else y + b


def to_chunks(t):
    b, s = t.shape[:2]
    return jnp.moveaxis(t.reshape((b, s // CHUNK, CHUNK) + t.shape[2:]), 1, 0)


def from_chunks(t):
    t = jnp.moveaxis(t, 0, 1)
    return t.reshape((t.shape[0], t.shape[1] * t.shape[2]) + t.shape[3:])


def mlstm_chunked(q, k, v, li, lf):
    b, s, H, dk = q.shape
    dv = v.shape[-1]
    tril = jnp.tril(jnp.ones((CHUNK, CHUNK), bool))

    def step(carry, inp):
        C, n, m = carry
        qc, kc, vc, ic, fc = inp
        cb = jnp.cumsum(fc, axis=1)
        D = cb[:, :, None, :] - cb[:, None, :, :] + ic[:, None, :, :]
        D = jnp.where(tril[None, :, :, None], D, -jnp.inf)
        inter = cb + m[:, None, :]
        mt = jnp.maximum(inter, jnp.max(D, axis=2))
        sc = jnp.einsum('bthd,bshd->btsh', qc, kc) * jnp.exp(D - mt[:, :, None, :])
        g = jnp.exp(inter - mt)
        num = jnp.einsum('btsh,bshv->bthv', sc, vc) + g[..., None] * jnp.einsum('bthd,bhdv->bthv', qc, C)
        den = jnp.sum(sc, axis=2) + g * jnp.einsum('bthd,bhd->bth', qc, n)
        h = num / jnp.maximum(jnp.abs(den), jnp.exp(-mt))[..., None]
        bl = cb[:, -1]
        wl = bl[:, None, :] - cb + ic
        m_new = jnp.maximum(bl + m, jnp.max(wl, axis=1))
        w = jnp.exp(wl - m_new[:, None, :])
        gs = jnp.exp(bl + m - m_new)
        C_new = gs[..., None, None] * C + jnp.einsum('bsh,bshd,bshv->bhdv', w, kc, vc)
        n_new = gs[..., None] * n + jnp.einsum('bsh,bshd->bhd', w, kc)
        return (C_new, n_new, m_new), h

    f32 = jnp.float32
    init = (jnp.zeros((b, H, dk, dv), f32), jnp.zeros((b, H, dk), f32), jnp.zeros((b, H), f32))
    _, h = lax.scan(step, init, (to_chunks(q), to_chunks(k), to_chunks(v), to_chunks(li), to_chunks(lf)))
    return from_chunks(h)


def gdn_chunked(q, k, v, beta, g):
    b, s, H, dk = q.shape
    dv = v.shape[-1]
    tril = jnp.tril(jnp.ones((CHUNK, CHUNK), bool))
    strict = jnp.tril(jnp.ones((CHUNK, CHUNK), bool), -1)

    def step(S, inp):
        qc, kc, vc, bc, gc = inp
        dec = jnp.cumsum(gc, axis=1)
        decT = jnp.swapaxes(dec, 1, 2)
        Lm = jnp.exp(jnp.where(tril, decT[..., :, None] - decT[..., None, :], -jnp.inf))
        kb = kc * bc[..., None]
        A = jnp.where(strict, jnp.einsum('bthd,bshd->bhts', kb, kc) * Lm, 0.0)
        M = A + jnp.eye(CHUNK, dtype=A.dtype)
        rhs = jnp.swapaxes(jnp.concatenate([vc * bc[..., None], kb * jnp.exp(dec)[..., None]], -1), 1, 2)
        sol = lax.linalg.triangular_solve(M, rhs, left_side=True, lower=True, unit_diagonal=True)
        u, w = sol[..., :dv], sol[..., dv:]
        v_new = u - jnp.einsum('bhtd,bhdv->bhtv', w, S)
        qk = jnp.einsum('bthd,bshd->bhts', qc, kc) * Lm
        o = jnp.einsum('bthd,bhdv->bhtv', qc * jnp.exp(dec)[..., None], S) + jnp.einsum('bhts,bhsv->bhtv', qk, v_new)
        kdec = jnp.swapaxes(kc * jnp.exp(dec[:, -1:] - dec)[..., None], 1, 2)
        S_new = S * jnp.exp(decT[..., -1])[..., None, None] + jnp.einsum('bhsd,bhsv->bhdv', kdec, v_new)
        return S_new, jnp.swapaxes(o, 1, 2)

    S0 = jnp.zeros((b, H, dk, dv), jnp.float32)
    _, o = lax.scan(step, S0, (to_chunks(q), to_chunks(k), to_chunks(v), to_chunks(beta), to_chunks(g)))
    return from_chunks(o)


def ssd_chunked(xdt, a, Bm, Cm):
    b, s, H, P = xdt.shape
    G, N = Bm.shape[2:]
    Hg = H // G
    xdt = xdt.reshape(b, s, G, Hg, P)
    a = a.reshape(b, s, G, Hg)
    tril = jnp.tril(jnp.ones((CHUNK, CHUNK), bool))

    def step(state, inp):
        xc, ac, Bc, Cc = inp
        cs = jnp.cumsum(ac, axis=1)
        Lm = jnp.exp(jnp.where(tril[None, :, :, None, None], cs[:, :, None] - cs[:, None], -jnp.inf))
        CB = jnp.einsum('btgn,bsgn->btsg', Cc, Bc)
        y = jnp.einsum('btsg,btsgh,bsghp->btghp', CB, Lm, xc)
        y = y + jnp.einsum('btgn,bghpn->btghp', Cc, state) * jnp.exp(cs)[..., None]
        new = state * jnp.exp(cs[:, -1])[..., None, None] + \
            jnp.einsum('bsgn,bsgh,bsghp->bghpn', Bc, jnp.exp(cs[:, -1:] - cs), xc)
        return new, y

    state0 = jnp.zeros((b, G, Hg, P, N), jnp.float32)
    _, y = lax.scan(step, state0, (to_chunks(xdt), to_chunks(a), to_chunks(Bm), to_chunks(Cm)))
    return from_chunks(y).reshape(b, s, H, P)


def even_mixer(u, w_in, ml_igate_b, ml_fgate_b, ml_norm_g, gdn_conv_w, gdn_A_log, gdn_dt_bias, gdn_norm_g, w_out):
    f32 = jnp.float32
    b, s, _ = u.shape
    mq, mk, mv, mo, mi, mf, gqkv, gz, gb, ga = split_cols(u @ w_in, EVEN_SPLITS)
    q = mq.astype(f32).reshape(b, s, ML_HEADS, ML_DK)
    k = mk.astype(f32).reshape(b, s, ML_HEADS, ML_DK) * (ML_DK ** -0.5)
    v = mv.astype(f32).reshape(b, s, ML_HEADS, ML_DV)
    li = soft_cap(mi.astype(f32) + ml_igate_b)
    lf = jax.nn.log_sigmoid(soft_cap(mf.astype(f32) + ml_fgate_b))
    ha = mlstm_chunked(q, k, v, li, lf)
    ha = (rms_normalize(ha).reshape(b, s, ML_V) * ml_norm_g) * jax.nn.sigmoid(mo.astype(f32))
    qkv = jax.nn.silu(causal_dwconv(gqkv, gdn_conv_w).astype(f32))
    gq, gk, gv = split_cols(qkv, (GDN_QK, GDN_QK, GDN_V))
    gq = l2norm(gq.reshape(b, s, GDN_HEADS, GDN_DK)) * (GDN_DK ** -0.5)
    gk = l2norm(gk.reshape(b, s, GDN_HEADS, GDN_DK))
    gv = gv.reshape(b, s, GDN_HEADS, GDN_DV)
    beta = jax.nn.sigmoid(gb.astype(f32))
    g = -jnp.exp(gdn_A_log.astype(f32)) * jax.nn.softplus(ga.astype(f32) + gdn_dt_bias)
    hb = gdn_chunked(gq, gk, gv, beta, g)
    hb = (rms_normalize(hb) * gdn_norm_g).reshape(b, s, GDN_V) * jax.nn.silu(gz.astype(f32))
    y = jnp.concatenate([ha, hb], axis=-1).astype(u.dtype)
    return y @ w_out


def mamba2_mixer(u, w_in, conv_w, conv_b, dt_bias, A_log, D_skip, norm_g, w_out):
    f32 = jnp.float32
    b, s, _ = u.shape
    z, xbc, dt = split_cols(u @ w_in, M2_SPLITS)
    xbc = jax.nn.silu(causal_dwconv(xbc, conv_w, conv_b).astype(f32))
    xs, Bm, Cm = split_cols(xbc, (M2_DINNER, M2_GROUPS * M2_DSTATE, M2_GROUPS * M2_DSTATE))
    dt = jax.nn.softplus(dt.astype(f32) + dt_bias)
    A = -jnp.exp(A_log.astype(f32))
    xs = xs.reshape(b, s, M2_HEADS, M2_HEADDIM)
    y = ssd_chunked(xs * dt[..., None], dt * A,
                    Bm.reshape(b, s, M2_GROUPS, M2_DSTATE), Cm.reshape(b, s, M2_GROUPS, M2_DSTATE))
    y = y + D_skip[:, None] * xs
    y = y.reshape(b, s, M2_DINNER) * jax.nn.silu(z.astype(f32))
    y = rms_normalize(y.reshape(b, s, M2_GROUPS, M2_DINNER // M2_GROUPS)).reshape(b, s, M2_DINNER) * norm_g
    return y.astype(u.dtype) @ w_out


def conv_ffn(u, w_up, conv_w, conv_b, w_down):
    gate, up = split_cols(u @ w_up, (FFN_DIM, FFN_DIM))
    gate = causal_dwconv(gate, conv_w, conv_b)
    return (jax.nn.silu(gate) * up) @ w_down


def setup_inputs(seed: int = 0) -> dict:
    key = jax.random.key(seed)
    ks = jax.random.split(key, 32)
    f32 = jnp.float32
    nrm = lambda k, shape, sc: jax.random.normal(k, shape, f32) * sc

    def dt_bias(k, shape):
        dt = jnp.exp(jax.random.uniform(k, shape, f32, math.log(1e-3), math.log(1e-1)))
        return dt + jnp.log(-jnp.expm1(-dt))

    def a_log(k, shape):
        return jnp.log(jax.random.uniform(k, shape, f32, 1.0, 16.0))

    return {
        "x": nrm(ks[0], (BATCH, SEQ, D_MODEL), 1.0),
        "norm_mix_g": 1.0 + nrm(ks[1], (DEPTH, D_MODEL), 0.02),
        "norm_ffn_g": 1.0 + nrm(ks[2], (DEPTH, D_MODEL), 0.02),
        "ev_w_in": nrm(ks[3], (N_EVEN, D_MODEL, EVEN_IN), D_MODEL ** -0.5),
        "ml_igate_b": nrm(ks[4], (N_EVEN, ML_HEADS), 0.1),
        "ml_fgate_b": 3.0 + nrm(ks[5], (N_EVEN, ML_HEADS), 0.5),
        "ml_norm_g": 1.0 + nrm(ks[6], (N_EVEN, ML_V), 0.02),
        "gdn_conv_w": nrm(ks[7], (N_EVEN, GDN_CONV, 2 * GDN_QK + GDN_V), GDN_CONV ** -0.5),
        "gdn_A_log": a_log(ks[8], (N_EVEN, GDN_HEADS)),
        "gdn_dt_bias": dt_bias(ks[9], (N_EVEN, GDN_HEADS)),
        "gdn_norm_g": 1.0 + nrm(ks[10], (N_EVEN, GDN_DV), 0.02),
        "ev_w_out": nrm(ks[11], (N_EVEN, MIX_WIDTH, D_MODEL), MIX_WIDTH ** -0.5),
        "m2_w_in": nrm(ks[12], (N_ODD, D_MODEL, M2_IN), D_MODEL ** -0.5),
        "m2_conv_w": nrm(ks[13], (N_ODD, M2_CONV, M2_XBC), M2_CONV ** -0.5),
        "m2_conv_b": nrm(ks[14], (N_ODD, M2_XBC), 0.02),
        "m2_dt_bias": dt_bias(ks[15], (N_ODD, M2_HEADS)),
        "m2_A_log": a_log(ks[16], (N_ODD, M2_HEADS)),
        "m2_D": 1.0 + nrm(ks[17], (N_ODD, M2_HEADS), 0.1),
        "m2_norm_g": 1.0 + nrm(ks[18], (N_ODD, M2_DINNER), 0.02),
        "m2_w_out": nrm(ks[19], (N_ODD, M2_DINNER, D_MODEL), M2_DINNER ** -0.5),
        "ffn_w_up": nrm(ks[20], (DEPTH, D_MODEL, 2 * FFN_DIM), D_MODEL ** -0.5),
        "ffn_conv_w": nrm(ks[21], (DEPTH, FFN_CONV, FFN_DIM), FFN_CONV ** -0.5),
        "ffn_conv_b": nrm(ks[22], (DEPTH, FFN_DIM), 0.02),
        "ffn_w_down": nrm(ks[23], (DEPTH, FFN_DIM, D_MODEL), FFN_DIM ** -0.5),
        "final_norm_g": 1.0 + nrm(ks[24], (D_MODEL,), 0.02),
    }


def reference(x, norm_mix_g, norm_ffn_g, ev_w_in, ml_igate_b, ml_fgate_b, ml_norm_g, gdn_conv_w,
              gdn_A_log, gdn_dt_bias, gdn_norm_g, ev_w_out, m2_w_in, m2_conv_w, m2_conv_b, m2_dt_bias,
              m2_A_log, m2_D, m2_norm_g, m2_w_out, ffn_w_up, ffn_conv_w, ffn_conv_b, ffn_w_down,
              final_norm_g):
    h = x
    for layer in range(DEPTH):
        u = rmsnorm(h, norm_mix_g[layer])
        if layer % 2 == 0:
            e = layer // 2
            mix = even_mixer(u, ev_w_in[e], ml_igate_b[e], ml_fgate_b[e], ml_norm_g[e], gdn_conv_w[e],
                             gdn_A_log[e], gdn_dt_bias[e], gdn_norm_g[e], ev_w_out[e])
        else:
            o = layer // 2
            mix = mamba2_mixer(u, m2_w_in[o], m2_conv_w[o], m2_conv_b[o], m2_dt_bias[o], m2_A_log[o],
                               m2_D[o], m2_norm_g[o], m2_w_out[o])
        h = h + mix.astype(h.dtype)
        f = conv_ffn(rmsnorm(h, norm_ffn_g[layer]), ffn_w_up[layer], ffn_conv_w[layer],
                     ffn_conv_b[layer], ffn_w_down[layer])
        h = h + f.astype(h.dtype)
    return rmsnorm(h, final_norm_g)
```

```python
import functools

import jax
import jax.numpy as jnp
from jax import lax
from jax.experimental import pallas as pl
from jax.experimental.pallas import tpu as pltpu

CHUNK = 128
EPS = 1e-6
GATE_CAP = 15.0
LANES = 128
SUBLANES = 8
VMEM_LIMIT_BYTES = 56 * 1024 * 1024

F32 = jnp.float32
BF16 = jnp.bfloat16
HIGHEST = lax.Precision.HIGHEST


def _params(*sem):
    return pltpu.CompilerParams(dimension_semantics=sem, vmem_limit_bytes=VMEM_LIMIT_BYTES)


def _softplus(x):
    return jnp.maximum(x, 0.0) + jnp.log1p(jnp.exp(-jnp.abs(x)))


def _log_sigmoid(x):
    return -_softplus(-x)


def _sigmoid(x):
    return 1.0 / (1.0 + jnp.exp(-x))


def _silu(x):
    return x * _sigmoid(x)


def _dot(a, b):
    return jnp.dot(a, b, preferred_element_type=F32)


def _dot_nt(a, b):
    return lax.dot_general(a, b, (((1,), (1,)), ((), ())), preferred_element_type=F32)


def _dot_f32(a, b):
    return jnp.dot(a, b, preferred_element_type=F32, precision=HIGHEST)


def _tri_masks(n):
    row = lax.broadcasted_iota(jnp.int32, (n, n), 0)
    col = lax.broadcasted_iota(jnp.int32, (n, n), 1)
    return row, col


def _pick_col(tile, idx):
    lane = lax.broadcasted_iota(jnp.int32, tile.shape, 1)
    return jnp.sum(jnp.where(lane == idx, tile, 0.0), axis=1, keepdims=True)


def _pick_row(tile, idx):
    sub = lax.broadcasted_iota(jnp.int32, tile.shape, 0)
    return jnp.sum(jnp.where(sub == idx, tile, 0.0), axis=0, keepdims=True)


def _rmsnorm_kernel(x_ref, g_ref, o_ref):
    x = x_ref[...]
    ms = jnp.mean(x * x, axis=-1, keepdims=True)
    o_ref[...] = ((x * lax.rsqrt(ms + EPS)) * g_ref[...]).astype(o_ref.dtype)


def _rmsnorm(x, g, out_dtype, tm=256):
    m, d = x.shape
    tm = min(tm, m)
    return pl.pallas_call(
        _rmsnorm_kernel,
        grid=(m // tm,),
        in_specs=[pl.BlockSpec((tm, d), lambda i: (i, 0)),
                  pl.BlockSpec((1, d), lambda i: (0, 0))],
        out_specs=pl.BlockSpec((tm, d), lambda i: (i, 0)),
        out_shape=jax.ShapeDtypeStruct((m, d), out_dtype),
        compiler_params=_params("parallel"),
    )(x, g.reshape(1, d))


def _mm_kernel(*refs, n_in, has_res, nk):
    o_ref = refs[-1]
    acc = _dot(refs[0][...], refs[1][...])
    for t in range(1, n_in):
        acc = acc + _dot(refs[2 * t][...], refs[2 * t + 1][...])
    res_ref = refs[2 * n_in] if has_res else None
    if nk == 1:
        if has_res:
            acc = acc + res_ref[...]
        o_ref[...] = acc.astype(o_ref.dtype)
    else:
        k = pl.program_id(2)

        @pl.when(k == 0)
        def _():
            o_ref[...] = (acc + res_ref[...]) if has_res else acc

        @pl.when(k > 0)
        def _():
            o_ref[...] += acc


def _matmul(xs, ws, res=None, *, tm=1024, tn=1024, tk=4096):
    m, kdim = xs[0].shape
    n = ws[0].shape[1]
    tm, tn, tk = min(tm, m), min(tn, n), min(tk, kdim)
    nk = kdim // tk
    assert m % tm == 0 and n % tn == 0 and kdim % tk == 0
    in_specs, args = [], []
    for x, w in zip(xs, ws):
        assert x.shape == (m, kdim) and w.shape == (kdim, n)
        in_specs += [pl.BlockSpec((tm, tk), lambda i, j, k: (i, k)),
                     pl.BlockSpec((tk, tn), lambda i, j, k: (k, j))]
        args += [x, w]
    if res is not None:
        in_specs.append(pl.BlockSpec((tm, tn), lambda i, j, k: (i, j)))
        args.append(res)
    return pl.pallas_call(
        functools.partial(_mm_kernel, n_in=len(xs), has_res=res is not None, nk=nk),
        grid=(m // tm, n // tn, nk),
        in_specs=in_specs,
        out_specs=pl.BlockSpec((tm, tn), lambda i, j, k: (i, j)),
        out_shape=jax.ShapeDtypeStruct((m, n), F32),
        compiler_params=_params("parallel", "parallel", "arbitrary"),
    )(*args)


def _conv_kernel(*refs, kw, has_bias, mode):
    x_ref, w_ref = refs[0], refs[1]
    pos = 2
    b_ref = None
    if has_bias:
        b_ref = refs[pos]
        pos += 1
    up_ref = None
    if mode == "ffn":
        up_ref = refs[pos]
        pos += 1
    o_ref, halo_ref = refs[pos], refs[pos + 1]
    s = pl.program_id(2)

    @pl.when(s == 0)
    def _():
        halo_ref[...] = jnp.zeros_like(halo_ref)

    x = x_ref[0]
    ts = x.shape[0]
    xcat = jnp.concatenate([halo_ref[...], x], axis=0)
    y = None
    for k in range(kw):
        off = SUBLANES - (kw - 1) + k
        term = w_ref[k:k + 1, :] * xcat[off:off + ts, :]
        y = term if y is None else y + term
    if has_bias:
        y = y + b_ref[...]
    halo_ref[...] = x[ts - SUBLANES:, :]
    if mode == "silu":
        o_ref[0] = _silu(y).astype(o_ref.dtype)
    else:
        o_ref[0] = (_silu(y) * up_ref[0]).astype(o_ref.dtype)


def _causal_conv(src, w, b, *, col0, ncols, mode, up_col0=None, out_dtype=F32, ts=512, tc=512):
    bsz, seq, _ = src.shape
    kw = w.shape[0]
    ts, tc = min(ts, seq), min(tc, ncols)
    assert seq % ts == 0 and ncols % tc == 0 and col0 % tc == 0
    off = col0 // tc
    in_specs = [pl.BlockSpec((1, ts, tc), lambda bi, j, s: (bi, s, j + off)),
                pl.BlockSpec((kw, tc), lambda bi, j, s: (0, j))]
    args = [src, w]
    if b is not None:
        in_specs.append(pl.BlockSpec((1, tc), lambda bi, j, s: (0, j)))
        args.append(b.reshape(1, ncols))
    if mode == "ffn":
        assert up_col0 % tc == 0
        uoff = up_col0 // tc
        in_specs.append(pl.BlockSpec((1, ts, tc), lambda bi, j, s: (bi, s, j + uoff)))
        args.append(src)
    return pl.pallas_call(
        functools.partial(_conv_kernel, kw=kw, has_bias=b is not None, mode=mode),
        grid=(bsz, ncols // tc, seq // ts),
        in_specs=in_specs,
        out_specs=pl.BlockSpec((1, ts, tc), lambda bi, j, s: (bi, s, j)),
        out_shape=jax.ShapeDtypeStruct((bsz, seq, ncols), out_dtype),
        scratch_shapes=[pltpu.VMEM((SUBLANES, tc), F32)],
        compiler_params=_params("parallel", "parallel", "arbitrary"),
    )(*args)


def _mlstm_kernel(q_ref, k_ref, v_ref, og_ref, gate_ref, gbias_ref, ng_ref, y_ref,
                  c_scr, n_scr, m_scr, *, n_heads):
    h = pl.program_id(1)
    c = pl.program_id(2)

    @pl.when(c == 0)
    def _():
        c_scr[...] = jnp.zeros_like(c_scr)
        n_scr[...] = jnp.zeros_like(n_scr)
        m_scr[...] = jnp.zeros_like(m_scr)

    L = CHUNK
    row, col = _tri_masks(L)
    tril = row >= col
    capped = GATE_CAP * jnp.tanh((gate_ref[0] + gbias_ref[...]) / GATE_CAP)
    log_f = _log_sigmoid(capped)
    cum = _dot_f32(tril.astype(F32), log_f)
    i_col = _pick_col(capped, h)
    cb_col = _pick_col(cum, n_heads + h)
    i_row = _pick_row(capped.T, h)
    cb_row = _pick_row(cum.T, n_heads + h)

    q = q_ref[0]
    dk = q.shape[-1]
    k = k_ref[0] * (dk ** -0.5)
    v = v_ref[0]
    qb, kb, vb = q.astype(BF16), k.astype(BF16), v.astype(BF16)
    m_prev = m_scr[...]

    d = jnp.where(tril, cb_col - cb_row + i_row, -jnp.inf)
    inter = cb_col + m_prev
    mt = jnp.maximum(inter, jnp.max(d, axis=1, keepdims=True))
    sc = _dot_nt(qb, kb) * jnp.exp(d - mt)
    g = jnp.exp(inter - mt)
    c_old = c_scr[...]
    n_old = n_scr[...]
    num = _dot(sc.astype(BF16), vb) + g * _dot(qb, c_old.astype(BF16))
    den = jnp.sum(sc, axis=1, keepdims=True) + g * jnp.sum(q * n_old, axis=1, keepdims=True)
    hh = num / jnp.maximum(jnp.abs(den), jnp.exp(-mt))

    bl = cb_col[L - 1:L, :]
    wl = bl - cb_col + i_col
    m_new = jnp.maximum(bl + m_prev, jnp.max(wl, axis=0, keepdims=True))
    kw = k * jnp.exp(wl - m_new)
    gs = jnp.exp(bl + m_prev - m_new)
    c_scr[...] = gs * c_old + _dot(kw.T.astype(BF16), vb)
    n_scr[...] = gs * n_old + jnp.sum(kw, axis=0, keepdims=True)
    m_scr[...] = m_new

    hn = hh * lax.rsqrt(jnp.mean(hh * hh, axis=-1, keepdims=True) + EPS)
    y_ref[0] = ((hn * ng_ref[...]) * _sigmoid(og_ref[0])).astype(y_ref.dtype)


def _mlstm(proj, gates, gbias, norm_g, *, n_heads, dk, dv, q_col0, k_col0, v_col0, o_col0):
    bsz, seq, _ = proj.shape
    nc = seq // CHUNK
    qo, ko, vo, oo = q_col0 // dk, k_col0 // dk, v_col0 // dv, o_col0 // dv
    return pl.pallas_call(
        functools.partial(_mlstm_kernel, n_heads=n_heads),
        grid=(bsz, n_heads, nc),
        in_specs=[pl.BlockSpec((1, CHUNK, dk), lambda b, h, c: (b, c, qo + h)),
                  pl.BlockSpec((1, CHUNK, dk), lambda b, h, c: (b, c, ko + h)),
                  pl.BlockSpec((1, CHUNK, dv), lambda b, h, c: (b, c, vo + h)),
                  pl.BlockSpec((1, CHUNK, dv), lambda b, h, c: (b, c, oo + h)),
                  pl.BlockSpec((1, CHUNK, LANES), lambda b, h, c: (b, c, 0)),
                  pl.BlockSpec((1, LANES), lambda b, h, c: (0, 0)),
                  pl.BlockSpec((1, dv), lambda b, h, c: (0, h))],
        out_specs=pl.BlockSpec((1, CHUNK, dv), lambda b, h, c: (b, c, h)),
        out_shape=jax.ShapeDtypeStruct((bsz, seq, n_heads * dv), BF16),
        scratch_shapes=[pltpu.VMEM((dk, dv), F32), pltpu.VMEM((1, dk), F32), pltpu.VMEM((1, 1), F32)],
        compiler_params=_params("parallel", "parallel", "arbitrary"),
    )(proj, proj, proj, proj, gates, gbias, norm_g.reshape(1, n_heads * dv))


def _unit_lower_inverse(a, row, col):
    n = a.shape[0]

    def same_block(size):
        return (row // size) == (col // size)

    base = min(16, n)
    ad = jnp.where(same_block(base), a, 0.0)
    t = jnp.where(row == col, 1.0, 0.0) - ad
    p = ad
    size = 2
    while size < base:
        p = _dot_f32(p, p)
        t = t + _dot_f32(t, p)
        size *= 2
    size = base
    while size < n:
        off = jnp.where(same_block(2 * size) & jnp.logical_not(same_block(size)), a, 0.0)
        t = t - _dot_f32(_dot_f32(t, off), t)
        size *= 2
    return t


def _gdn_kernel(q_ref, k_ref, v_ref, z_ref, gate_ref, gbias_ref, alog_ref, ng_ref, y_ref, s_scr,
                *, beta_col0, g_col0):
    h = pl.program_id(1)
    c = pl.program_id(2)

    @pl.when(c == 0)
    def _():
        s_scr[...] = jnp.zeros_like(s_scr)

    L = CHUNK
    row, col = _tri_masks(L)
    tril = row >= col
    pre = gate_ref[0]
    beta_t = _sigmoid(pre)
    g_t = -jnp.exp(alog_ref[...]) * _softplus(pre + gbias_ref[...])
    cum = _dot_f32(tril.astype(F32), g_t)
    beta = _pick_col(beta_t, beta_col0 + h)
    dec_col = _pick_col(cum, g_col0 + h)
    dec_row = _pick_row(cum.T, g_col0 + h)

    q = q_ref[0]
    k = k_ref[0]
    v = v_ref[0]
    dk = q.shape[-1]
    qn = (q * lax.rsqrt(jnp.sum(q * q, axis=-1, keepdims=True) + EPS)) * (dk ** -0.5)
    kn = k * lax.rsqrt(jnp.sum(k * k, axis=-1, keepdims=True) + EPS)
    knb = kn.astype(BF16)

    lm = jnp.exp(jnp.where(tril, dec_col - dec_row, -jnp.inf))
    kbeta = kn * beta
    a = jnp.where(row > col, _dot_nt(kbeta.astype(BF16), knb) * lm, 0.0)
    t = _unit_lower_inverse(a, row, col)
    ed = jnp.exp(dec_col)
    rhs = jnp.concatenate([v * beta, kbeta * ed], axis=-1)
    sol = _dot_f32(t, rhs)
    dv = v.shape[-1]
    u, w = sol[:, :dv], sol[:, dv:]
    s_old = s_scr[...]
    sb = s_old.astype(BF16)
    v_new = u - _dot(w.astype(BF16), sb)
    vnb = v_new.astype(BF16)
    qk = _dot_nt(qn.astype(BF16), knb) * lm
    o = _dot((qn * ed).astype(BF16), sb) + _dot(qk.astype(BF16), vnb)
    dlast = dec_col[L - 1:L, :]
    kdec = kn * jnp.exp(dlast - dec_col)
    s_scr[...] = s_old * jnp.exp(dlast) + _dot(kdec.T.astype(BF16), vnb)

    on = o * lax.rsqrt(jnp.mean(o * o, axis=-1, keepdims=True) + EPS)
    y_ref[0] = ((on * ng_ref[...]) * _silu(z_ref[0])).astype(y_ref.dtype)


def _gdn(qkv, proj, gates, gbias, alog_row, norm_g, *, n_heads, dk, dv, z_col0, beta_col0, g_col0):
    bsz, seq, _ = qkv.shape
    nc = seq // CHUNK
    assert dk == dv
    zo = z_col0 // dv
    return pl.pallas_call(
        functools.partial(_gdn_kernel, beta_col0=beta_col0, g_col0=g_col0),
        grid=(bsz, n_heads, nc),
        in_specs=[pl.BlockSpec((1, CHUNK, dk), lambda b, h, c: (b, c, h)),
                  pl.BlockSpec((1, CHUNK, dk), lambda b, h, c: (b, c, n_heads + h)),
                  pl.BlockSpec((1, CHUNK, dv), lambda b, h, c: (b, c, 2 * n_heads + h)),
                  pl.BlockSpec((1, CHUNK, dv), lambda b, h, c: (b, c, zo + h)),
                  pl.BlockSpec((1, CHUNK, LANES), lambda b, h, c: (b, c, 0)),
                  pl.BlockSpec((1, LANES), lambda b, h, c: (0, 0)),
                  pl.BlockSpec((1, LANES), lambda b, h, c: (0, 0)),
                  pl.BlockSpec((1, dv), lambda b, h, c: (0, 0))],
        out_specs=pl.BlockSpec((1, CHUNK, dv), lambda b, h, c: (b, c, h)),
        out_shape=jax.ShapeDtypeStruct((bsz, seq, n_heads * dv), BF16),
        scratch_shapes=[pltpu.VMEM((dk, dv), F32)],
        compiler_params=_params("parallel", "parallel", "arbitrary"),
    )(qkv, qkv, qkv, proj, gates, gbias, alog_row, norm_g.reshape(1, dv))


def _ssd_kernel(x_ref, b_ref, c_ref, z_ref, dt_ref, dtb_ref, alog_ref, dskip_ref, ng_ref, y_ref,
                st_scr, *, hg, hp):
    g = pl.program_id(1)
    c = pl.program_id(2)

    @pl.when(c == 0)
    def _():
        st_scr[...] = jnp.zeros_like(st_scr)

    L = CHUNK
    row, col = _tri_masks(L)
    tril = row >= col
    dt_all = _softplus(dt_ref[0] + dtb_ref[...])
    a_all = dt_all * (-jnp.exp(alog_ref[...]))
    cs_all = _dot_f32(tril.astype(F32), a_all)
    shift = (LANES - g * hg) % LANES
    dt_g = pltpu.roll(dt_all, shift, axis=1)
    cs_g = pltpu.roll(cs_all, shift, axis=1)
    cs_gt = cs_g.T

    bm = b_ref[0]
    cm = c_ref[0]
    bb, cmb = bm.astype(BF16), cm.astype(BF16)
    cb = _dot_nt(cmb, bb)
    x = x_ref[0]
    st_old = st_scr[...]
    y_inter = _dot(cmb, st_old.astype(BF16))
    cs_last = cs_g[L - 1:L, :]
    lane = lax.broadcasted_iota(jnp.int32, (L, LANES), 1)
    lo = lane < hp
    lane1 = lax.broadcasted_iota(jnp.int32, (1, LANES), 1)
    lo1 = lane1 < hp

    y_parts, xw_parts, decay_parts = [], [], []
    for p in range(hg // 2):
        j0, j1 = 2 * p, 2 * p + 1
        xp = x[:, p * LANES:(p + 1) * LANES]
        dt_p = jnp.where(lo, dt_g[:, j0:j0 + 1], dt_g[:, j1:j1 + 1])
        cs_p = jnp.where(lo, cs_g[:, j0:j0 + 1], cs_g[:, j1:j1 + 1])
        last_p = jnp.where(lo1, cs_last[:, j0:j0 + 1], cs_last[:, j1:j1 + 1])
        xdt = xp * dt_p
        y_p = y_inter[:, p * LANES:(p + 1) * LANES] * jnp.exp(cs_p)
        for j, keep in ((j0, lo), (j1, jnp.logical_not(lo))):
            lm = jnp.exp(jnp.where(tril, cs_g[:, j:j + 1] - cs_gt[j:j + 1, :], -jnp.inf))
            mj = (cb * lm).astype(BF16)
            y_p = y_p + _dot(mj, jnp.where(keep, xdt, 0.0).astype(BF16))
        y_parts.append(y_p)
        xw_parts.append((xdt * jnp.exp(last_p - cs_p)).astype(BF16))
        decay_parts.append(jnp.exp(last_p))
    y = jnp.concatenate(y_parts, axis=1)
    xw = jnp.concatenate(xw_parts, axis=1)
    decay = jnp.concatenate(decay_parts, axis=1)
    st_scr[...] = st_old * decay + _dot(bm.T.astype(BF16), xw)

    y = (y + dskip_ref[...] * x) * _silu(z_ref[0])
    yn = y * lax.rsqrt(jnp.mean(y * y, axis=-1, keepdims=True) + EPS)
    y_ref[0] = (yn * ng_ref[...]).astype(y_ref.dtype)


def _ssd(xbc, zsrc, dtsrc, dt_bias, a_log, d_skip, norm_g, *, n_groups, hg, hp, ds):
    bsz, seq, _ = xbc.shape
    nc = seq // CHUNK
    gw = hg * hp
    d_inner = n_groups * gw
    assert hp * 2 == LANES and hg % 2 == 0 and n_groups * hg == LANES and ds == LANES
    bo = d_inner // ds
    dskip_row = jnp.repeat(d_skip, hp).reshape(1, d_inner)
    return pl.pallas_call(
        functools.partial(_ssd_kernel, hg=hg, hp=hp),
        grid=(bsz, n_groups, nc),
        in_specs=[pl.BlockSpec((1, CHUNK, gw), lambda b, g, c: (b, c, g)),
                  pl.BlockSpec((1, CHUNK, ds), lambda b, g, c: (b, c, bo + g)),
                  pl.BlockSpec((1, CHUNK, ds), lambda b, g, c: (b, c, bo + n_groups + g)),
                  pl.BlockSpec((1, CHUNK, gw), lambda b, g, c: (b, c, g)),
                  pl.BlockSpec((1, CHUNK, LANES), lambda b, g, c: (b, c, 0)),
                  pl.BlockSpec((1, LANES), lambda b, g, c: (0, 0)),
                  pl.BlockSpec((1, LANES), lambda b, g, c: (0, 0)),
                  pl.BlockSpec((1, gw), lambda b, g, c: (0, g)),
                  pl.BlockSpec((1, gw), lambda b, g, c: (0, g))],
        out_specs=pl.BlockSpec((1, CHUNK, gw), lambda b, g, c: (b, c, g)),
        out_shape=jax.ShapeDtypeStruct((bsz, seq, d_inner), BF16),
        scratch_shapes=[pltpu.VMEM((ds, gw), F32)],
        compiler_params=_params("parallel", "parallel", "arbitrary"),
    )(xbc, xbc, xbc, zsrc, dtsrc, dt_bias.reshape(1, LANES), a_log.reshape(1, LANES),
      dskip_row, norm_g.reshape(1, d_inner))


def _pad_row(parts, width=LANES):
    row = jnp.concatenate([p.astype(F32).reshape(-1) for p in parts])
    return jnp.pad(row, (0, width - row.shape[0])).reshape(1, width)


def _even_mixer(h2, u, bsz, seq, w_in, ml_igate_b, ml_fgate_b, ml_norm_g, gdn_conv_w, gdn_A_log,
                gdn_dt_bias, gdn_norm_g, w_out, dims):
    ml_heads, ml_dk, ml_dv, gdn_heads, gdn_dk, gdn_dv = dims
    ml_qk, ml_v = ml_heads * ml_dk, ml_heads * ml_dv
    gdn_qk, gdn_v = gdn_heads * gdn_dk, gdn_heads * gdn_dv
    qkv_w = 2 * gdn_qk + gdn_v
    c_gate = 2 * ml_qk + 2 * ml_v
    c_qkv = c_gate + 2 * ml_heads
    c_gb = c_qkv + qkv_w + gdn_v
    w_big = jnp.concatenate([w_in[:, :c_gate], w_in[:, c_qkv:c_gb]], axis=1).astype(BF16)
    n_small = 2 * ml_heads + 2 * gdn_heads
    w_small = jnp.concatenate([w_in[:, c_gate:c_qkv], w_in[:, c_gb:]], axis=1)
    w_small = jnp.pad(w_small, ((0, 0), (0, LANES - n_small))).astype(BF16)
    proj = _matmul([u], [w_big]).reshape(bsz, seq, -1)
    gates = _matmul([u], [w_small]).reshape(bsz, seq, LANES)
    gbias = _pad_row([ml_igate_b, ml_fgate_b, jnp.zeros((gdn_heads,), F32), gdn_dt_bias])
    alog_row = _pad_row([jnp.zeros((2 * ml_heads + gdn_heads,), F32), gdn_A_log])

    ya = _mlstm(proj, gates, gbias, ml_norm_g, n_heads=ml_heads, dk=ml_dk, dv=ml_dv,
                q_col0=0, k_col0=ml_qk, v_col0=2 * ml_qk, o_col0=2 * ml_qk + ml_v)
    qkv = _causal_conv(proj, gdn_conv_w, None, col0=c_gate, ncols=qkv_w, mode="silu")
    yb = _gdn(qkv, proj, gates, gbias, alog_row, gdn_norm_g, n_heads=gdn_heads, dk=gdn_dk, dv=gdn_dv,
              z_col0=c_gate + qkv_w, beta_col0=2 * ml_heads, g_col0=2 * ml_heads + gdn_heads)
    m = bsz * seq
    w_out = w_out.astype(BF16)
    return _matmul([ya.reshape(m, ml_v), yb.reshape(m, gdn_v)], [w_out[:ml_v], w_out[ml_v:]],
                   res=h2, tn=512)


def _mamba2_mixer(h2, u, bsz, seq, w_in, conv_w, conv_b, dt_bias, a_log, d_skip, norm_g, w_out, dims):
    d_inner, n_heads, n_groups, ds = dims
    hp = d_inner // n_heads
    hg = n_heads // n_groups
    xbc_w = d_inner + 2 * n_groups * ds
    w_in = w_in.astype(BF16)
    z = _matmul([u], [w_in[:, :d_inner]]).reshape(bsz, seq, d_inner)
    xbc = _matmul([u], [w_in[:, d_inner:d_inner + xbc_w]]).reshape(bsz, seq, xbc_w)
    dt = _matmul([u], [w_in[:, d_inner + xbc_w:]]).reshape(bsz, seq, n_heads)
    xbc = _causal_conv(xbc, conv_w, conv_b, col0=0, ncols=xbc_w, mode="silu")
    y = _ssd(xbc, z, dt, dt_bias, a_log, d_skip, norm_g, n_groups=n_groups, hg=hg, hp=hp, ds=ds)
    return _matmul([y.reshape(bsz * seq, d_inner)], [w_out.astype(BF16)], res=h2, tn=512)


def _conv_ffn(h2, u, bsz, seq, w_up, conv_w, conv_b, w_down):
    ffn = w_down.shape[0]
    gu = _matmul([u], [w_up.astype(BF16)]).reshape(bsz, seq, 2 * ffn)
    act = _causal_conv(gu, conv_w, conv_b, col0=0, ncols=ffn, mode="ffn", up_col0=ffn, out_dtype=BF16)
    return _matmul([act.reshape(bsz * seq, ffn)], [w_down.astype(BF16)], res=h2, tn=512)


def _trunk(x, norm_mix_g, norm_ffn_g, ev_w_in, ml_igate_b, ml_fgate_b, ml_norm_g, gdn_conv_w,
           gdn_A_log, gdn_dt_bias, gdn_norm_g, ev_w_out, m2_w_in, m2_conv_w, m2_conv_b, m2_dt_bias,
           m2_A_log, m2_D, m2_norm_g, m2_w_out, ffn_w_up, ffn_conv_w, ffn_conv_b, ffn_w_down,
           final_norm_g):
    bsz, seq, d = x.shape
    depth = norm_mix_g.shape[0]
    ml_heads = ml_igate_b.shape[1]
    ml_dv = ml_norm_g.shape[1] // ml_heads
    gdn_heads = gdn_A_log.shape[1]
    gdn_dv = gdn_norm_g.shape[1]
    even_dims = (ml_heads, ml_dv // 2, ml_dv, gdn_heads, gdn_dv, gdn_dv)
    m2_heads = m2_dt_bias.shape[1]
    m2_inner = m2_norm_g.shape[1]
    m2_xbc = m2_conv_w.shape[2]
    m2_groups = 8
    m2_ds = (m2_xbc - m2_inner) // (2 * m2_groups)
    m2_dims = (m2_inner, m2_heads, m2_groups, m2_ds)

    h2 = x.reshape(bsz * seq, d)
    for layer in range(depth):
        u = _rmsnorm(h2, norm_mix_g[layer], BF16)
        if layer % 2 == 0:
            e = layer // 2
            h2 = _even_mixer(h2, u, bsz, seq, ev_w_in[e], ml_igate_b[e], ml_fgate_b[e], ml_norm_g[e],
                             gdn_conv_w[e], gdn_A_log[e], gdn_dt_bias[e], gdn_norm_g[e], ev_w_out[e],
                             even_dims)
        else:
            o = layer // 2
            h2 = _mamba2_mixer(h2, u, bsz, seq, m2_w_in[o], m2_conv_w[o], m2_conv_b[o], m2_dt_bias[o],
                               m2_A_log[o], m2_D[o], m2_norm_g[o], m2_w_out[o], m2_dims)
        u = _rmsnorm(h2, norm_ffn_g[layer], BF16)
        h2 = _conv_ffn(h2, u, bsz, seq, ffn_w_up[layer], ffn_conv_w[layer], ffn_conv_b[layer],
                       ffn_w_down[layer])
    return _rmsnorm(h2, final_norm_g, x.dtype).reshape(bsz, seq, d)


def kernel(x, norm_mix_g, norm_ffn_g, ev_w_in, ml_igate_b, ml_fgate_b, ml_norm_g, gdn_conv_w, gdn_A_log, gdn_dt_bias, gdn_norm_g, ev_w_out, m2_w_in, m2_conv_w, m2_conv_b, m2_dt_bias, m2_A_log, m2_D, m2_norm_g, m2_w_out, ffn_w_up, ffn_conv_w, ffn_conv_b, ffn_w_down, final_norm_g):
    return _trunk(x, norm_mix_g, norm_ffn_g, ev_w_in, ml_igate_b, ml_fgate_b, ml_norm_g, gdn_conv_w,
                  gdn_A_log, gdn_dt_bias, gdn_norm_g, ev_w_out, m2_w_in, m2_conv_w, m2_conv_b,
                  m2_dt_bias, m2_A_log, m2_D, m2_norm_g, m2_w_out, ffn_w_up, ffn_conv_w, ffn_conv_b,
                  ffn_w_down, final_norm_g)
```

```python
import functools

import jax
import jax.numpy as jnp
from jax import lax
from jax.experimental import pallas as pl
from jax.experimental.pallas import tpu as pltpu

CHUNK = 128
EPS = 1e-6
GATE_CAP = 15.0
LANES = 128
SUBLANES = 8
VMEM_LIMIT_BYTES = 58 * 1024 * 1024
M2_GROUPS = 8
GDN_HEADS_PER_STEP = 8
CONV_ROW_SUBTILE = 256
MM_ROW_SUBTILE = 256

F32 = jnp.float32
BF16 = jnp.bfloat16
HIGHEST = lax.Precision.HIGHEST


def _params(*sem):
    return pltpu.CompilerParams(dimension_semantics=sem, vmem_limit_bytes=VMEM_LIMIT_BYTES)


def _softplus(x):
    return jnp.maximum(x, 0.0) + jnp.log1p(jnp.exp(-jnp.abs(x)))


def _log_sigmoid(x):
    return -_softplus(-x)


def _sigmoid(x):
    return 1.0 / (1.0 + jnp.exp(-x))


def _silu(x):
    return x * _sigmoid(x)


def _dot(a, b):
    return jnp.dot(a, b, preferred_element_type=F32)


def _dot_nt(a, b):
    return lax.dot_general(a, b, (((1,), (1,)), ((), ())), preferred_element_type=F32)


def _dot_f32(a, b):
    return jnp.dot(a, b, preferred_element_type=F32, precision=HIGHEST)


def _tri_masks(n):
    row = lax.broadcasted_iota(jnp.int32, (n, n), 0)
    col = lax.broadcasted_iota(jnp.int32, (n, n), 1)
    return row, col


def _pick_col(tile, idx):
    lane = lax.broadcasted_iota(jnp.int32, tile.shape, 1)
    return jnp.sum(jnp.where(lane == idx, tile, 0.0), axis=1, keepdims=True)


def _pick_row(tile, idx):
    sub = lax.broadcasted_iota(jnp.int32, tile.shape, 0)
    return jnp.sum(jnp.where(sub == idx, tile, 0.0), axis=0, keepdims=True)


def _rmsnorm_kernel(x_ref, g_ref, o_ref):
    x = x_ref[...]
    ms = jnp.mean(x * x, axis=-1, keepdims=True)
    o_ref[...] = ((x * lax.rsqrt(ms + EPS)) * g_ref[...]).astype(o_ref.dtype)


def _rmsnorm(x, g, out_dtype, tm=256):
    m, d = x.shape
    tm = min(tm, m)
    return pl.pallas_call(
        _rmsnorm_kernel,
        grid=(m // tm,),
        in_specs=[pl.BlockSpec((tm, d), lambda i: (i, 0)),
                  pl.BlockSpec((1, d), lambda i: (0, 0))],
        out_specs=pl.BlockSpec((tm, d), lambda i: (i, 0)),
        out_shape=jax.ShapeDtypeStruct((m, d), out_dtype),
        compiler_params=_params("parallel"),
        name="rmsnorm",
    )(x, g.reshape(1, d))


def _mm_kernel(*refs, n_in, has_res, nk, rs):
    o_ref = refs[-1]
    res_ref = refs[2 * n_in] if has_res else None

    def sweep(first):
        for r in range(o_ref.shape[0] // rs):
            rows = slice(r * rs, (r + 1) * rs)
            acc = _dot(refs[0][rows, :], refs[1][...])
            for t in range(1, n_in):
                acc = acc + _dot(refs[2 * t][rows, :], refs[2 * t + 1][...])
            if first:
                o_ref[rows, :] = (acc + res_ref[rows, :]) if has_res else acc
            else:
                o_ref[rows, :] += acc

    if nk == 1:
        sweep(True)
    else:
        k = pl.program_id(2)
        pl.when(k == 0)(functools.partial(sweep, True))
        pl.when(k > 0)(functools.partial(sweep, False))


def _matmul(xs, w, layer, res=None, *, row0s=(0,), col0=0, n=None, tm=1024, tn=1024, tk=4096, name):
    m, kdim = xs[0].shape
    n = w.shape[2] - col0 if n is None else n
    tm, tn, tk = min(tm, m), min(tn, n), min(tk, kdim)
    nk = kdim // tk
    assert m % tm == 0 and n % tn == 0 and kdim % tk == 0 and col0 % tn == 0
    coff = col0 // tn
    in_specs, args = [], []
    for x, row0 in zip(xs, row0s):
        assert x.shape == (m, kdim) and row0 % tk == 0
        roff = row0 // tk
        in_specs += [pl.BlockSpec((tm, tk), lambda i, j, k: (i, k)),
                     pl.BlockSpec((None, tk, tn), lambda i, j, k, roff=roff: (layer, k + roff, j + coff))]
        args += [x, w]
    if res is not None:
        in_specs.append(pl.BlockSpec((tm, tn), lambda i, j, k: (i, j)))
        args.append(res)
    rs = min(MM_ROW_SUBTILE, tm) if (res is not None or nk > 1) else tm
    assert tm % rs == 0
    return pl.pallas_call(
        functools.partial(_mm_kernel, n_in=len(xs), has_res=res is not None, nk=nk, rs=rs),
        grid=(m // tm, n // tn, nk),
        in_specs=in_specs,
        out_specs=pl.BlockSpec((tm, tn), lambda i, j, k: (i, j)),
        out_shape=jax.ShapeDtypeStruct((m, n), F32),
        compiler_params=_params("parallel", "parallel", "arbitrary"),
        name=name,
    )(*args)


def _mm_conv_kernel(*refs, kw, has_bias, gated, tiles_per_seq, rs):
    x_ref, w_ref = refs[0], refs[1]
    pos = 2
    wu_ref = None
    if gated:
        wu_ref = refs[pos]
        pos += 1
    cw_ref = refs[pos]
    pos += 1
    cb_ref = None
    if has_bias:
        cb_ref = refs[pos]
        pos += 1
    o_ref, halo_ref = refs[pos], refs[pos + 1]
    i = pl.program_id(1)

    @pl.when(i % tiles_per_seq == 0)
    def _():
        halo_ref[...] = jnp.zeros_like(halo_ref)

    prev = halo_ref[...]
    for r in range(x_ref.shape[0] // rs):
        xs = x_ref[r * rs:(r + 1) * rs, :]
        acc = _dot(xs, w_ref[...])
        cat = jnp.concatenate([prev, acc], axis=0)
        y = None
        for k in range(kw):
            off = SUBLANES - (kw - 1) + k
            term = cw_ref[k:k + 1, :] * cat[off:off + rs, :]
            y = term if y is None else y + term
        if has_bias:
            y = y + cb_ref[...]
        prev = acc[rs - SUBLANES:, :]
        act = _silu(y)
        if gated:
            act = act * _dot(xs, wu_ref[...])
        o_ref[r * rs:(r + 1) * rs, :] = act.astype(o_ref.dtype)
    halo_ref[...] = prev


def _matmul_conv(x, w, layer, conv_w, conv_b, *, seq, col0=0, n, up_col0=None, out_dtype=F32,
                 tm=1024, tn=1024, name):
    m, kdim = x.shape
    kw = conv_w.shape[0]
    tm, tn = min(tm, seq), min(tn, n)
    rs = min(CONV_ROW_SUBTILE, tm)
    assert seq % tm == 0 and n % tn == 0 and col0 % tn == 0 and tm % rs == 0 and kw - 1 <= SUBLANES
    coff = col0 // tn
    in_specs = [pl.BlockSpec((tm, kdim), lambda j, i: (i, 0)),
                pl.BlockSpec((None, kdim, tn), lambda j, i: (layer, 0, j + coff))]
    args = [x, w]
    gated = up_col0 is not None
    if gated:
        assert up_col0 % tn == 0
        uoff = up_col0 // tn
        in_specs.append(pl.BlockSpec((None, kdim, tn), lambda j, i: (layer, 0, j + uoff)))
        args.append(w)
    in_specs.append(pl.BlockSpec((kw, tn), lambda j, i: (0, j)))
    args.append(conv_w)
    if conv_b is not None:
        in_specs.append(pl.BlockSpec((1, tn), lambda j, i: (0, j)))
        args.append(conv_b.reshape(1, n))
    return pl.pallas_call(
        functools.partial(_mm_conv_kernel, kw=kw, has_bias=conv_b is not None, gated=gated,
                          tiles_per_seq=seq // tm, rs=rs),
        grid=(n // tn, m // tm),
        in_specs=in_specs,
        out_specs=pl.BlockSpec((tm, tn), lambda j, i: (i, j)),
        out_shape=jax.ShapeDtypeStruct((m, n), out_dtype),
        scratch_shapes=[pltpu.VMEM((SUBLANES, tn), F32)],
        compiler_params=_params("arbitrary", "arbitrary"),
        name=name,
    )(*args)


def _mlstm_kernel(q_ref, k_ref, v_ref, og_ref, gate_ref, gbias_ref, ng_ref, y_ref,
                  c_scr, n_scr, m_scr, *, n_heads):
    h = pl.program_id(1)
    c = pl.program_id(2)

    @pl.when(c == 0)
    def _():
        c_scr[...] = jnp.zeros_like(c_scr)
        n_scr[...] = jnp.zeros_like(n_scr)
        m_scr[...] = jnp.zeros_like(m_scr)

    L = CHUNK
    row, col = _tri_masks(L)
    tril = row >= col
    capped = GATE_CAP * jnp.tanh((gate_ref[0] + gbias_ref[...]) / GATE_CAP)
    log_f = _log_sigmoid(capped)
    cum = _dot_f32(tril.astype(F32), log_f)
    i_col = _pick_col(capped, h)
    cb_col = _pick_col(cum, n_heads + h)
    i_row = _pick_row(capped.T, h)
    cb_row = _pick_row(cum.T, n_heads + h)

    q = q_ref[0]
    dk = q.shape[-1]
    k = k_ref[0] * (dk ** -0.5)
    v = v_ref[0]
    qb, kb, vb = q.astype(BF16), k.astype(BF16), v.astype(BF16)
    m_prev = m_scr[...]

    d = jnp.where(tril, cb_col - cb_row + i_row, -jnp.inf)
    inter = cb_col + m_prev
    mt = jnp.maximum(inter, jnp.max(d, axis=1, keepdims=True))
    sc = _dot_nt(qb, kb) * jnp.exp(d - mt)
    g = jnp.exp(inter - mt)
    c_old = c_scr[...]
    n_old = n_scr[...]
    num = _dot(sc.astype(BF16), vb) + g * _dot(qb, c_old.astype(BF16))
    den = jnp.sum(sc, axis=1, keepdims=True) + g * jnp.sum(q * n_old, axis=1, keepdims=True)
    hh = num / jnp.maximum(jnp.abs(den), jnp.exp(-mt))

    bl = cb_col[L - 1:L, :]
    wl = bl - cb_col + i_col
    m_new = jnp.maximum(bl + m_prev, jnp.max(wl, axis=0, keepdims=True))
    kw = k * jnp.exp(wl - m_new)
    gs = jnp.exp(bl + m_prev - m_new)
    c_scr[...] = gs * c_old + _dot(kw.T.astype(BF16), vb)
    n_scr[...] = gs * n_old + jnp.sum(kw, axis=0, keepdims=True)
    m_scr[...] = m_new

    hn = hh * lax.rsqrt(jnp.mean(hh * hh, axis=-1, keepdims=True) + EPS)
    y_ref[0] = ((hn * ng_ref[...]) * _sigmoid(og_ref[0])).astype(y_ref.dtype)


def _mlstm(proj, gates, gbias, norm_g, *, n_heads, dk, dv, q_col0, k_col0, v_col0, o_col0):
    bsz, seq, _ = proj.shape
    nc = seq // CHUNK
    qo, ko, vo, oo = q_col0 // dk, k_col0 // dk, v_col0 // dv, o_col0 // dv
    return pl.pallas_call(
        functools.partial(_mlstm_kernel, n_heads=n_heads),
        grid=(bsz, n_heads, nc),
        in_specs=[pl.BlockSpec((1, CHUNK, dk), lambda b, h, c: (b, c, qo + h)),
                  pl.BlockSpec((1, CHUNK, dk), lambda b, h, c: (b, c, ko + h)),
                  pl.BlockSpec((1, CHUNK, dv), lambda b, h, c: (b, c, vo + h)),
                  pl.BlockSpec((1, CHUNK, dv), lambda b, h, c: (b, c, oo + h)),
                  pl.BlockSpec((1, CHUNK, LANES), lambda b, h, c: (b, c, 0)),
                  pl.BlockSpec((1, LANES), lambda b, h, c: (0, 0)),
                  pl.BlockSpec((1, dv), lambda b, h, c: (0, h))],
        out_specs=pl.BlockSpec((1, CHUNK, dv), lambda b, h, c: (b, c, h)),
        out_shape=jax.ShapeDtypeStruct((bsz, seq, n_heads * dv), BF16),
        scratch_shapes=[pltpu.VMEM((dk, dv), F32), pltpu.VMEM((1, dk), F32), pltpu.VMEM((1, 1), F32)],
        compiler_params=_params("parallel", "parallel", "arbitrary"),
        name="mlstm",
    )(proj, proj, proj, proj, gates, gbias, norm_g.reshape(1, n_heads * dv))


def _split_bf16(x):
    hi = x.astype(BF16)
    lo = (x - hi.astype(F32)).astype(BF16)
    return hi, lo


def _dot_split(a, b):
    n = a.shape[0]
    ah, al = _split_bf16(a)
    bh, bl = _split_bf16(b)
    rb = jnp.concatenate([bh, bl], axis=1)
    r = _dot(jnp.concatenate([ah, al], axis=1), jnp.concatenate([rb, rb], axis=0))
    return r[:, :n] + r[:, n:]


def _unit_lower_inverses(mats, row, col):
    n = mats[0].shape[0]

    def same_block(size):
        return (row // size) == (col // size)

    base = min(16, n)
    eye = jnp.where(row == col, 1.0, 0.0)
    in_base = same_block(base)
    ps = [jnp.where(in_base, a, 0.0) for a in mats]
    ts = [eye - p for p in ps]
    size = 2
    while size < base:
        ps = [_dot_split(p, p) for p in ps]
        ts = [t + _dot_split(t, p) for t, p in zip(ts, ps)]
        size *= 2
    size = base
    while size < n:
        ring = same_block(2 * size) & jnp.logical_not(same_block(size))
        tos = [_dot_split(t, jnp.where(ring, a, 0.0)) for t, a in zip(ts, mats)]
        ts = [t - _dot_split(to, t) for t, to in zip(ts, tos)]
        size *= 2
    return ts


def _gdn_kernel(q_ref, k_ref, v_ref, z_ref, gate_ref, gbias_ref, alog_ref, ng_ref, y_ref, s_scr,
                *, hb, dk, beta_col0, g_col0):
    hgrp = pl.program_id(1)
    c = pl.program_id(2)

    @pl.when(c == 0)
    def _():
        s_scr[...] = jnp.zeros_like(s_scr)

    L = CHUNK
    row, col = _tri_masks(L)
    tril = row >= col
    pre = gate_ref[0]
    beta_t = _sigmoid(pre)
    g_t = -jnp.exp(alog_ref[...]) * _softplus(pre + gbias_ref[...])
    cum = _dot_f32(tril.astype(F32), g_t)
    cum_t = cum.T

    heads = range(hb)
    cols = [slice(i * dk, (i + 1) * dk) for i in heads]
    beta = [_pick_col(beta_t, beta_col0 + hgrp * hb + i) for i in heads]
    dec_col = [_pick_col(cum, g_col0 + hgrp * hb + i) for i in heads]
    dec_row = [_pick_row(cum_t, g_col0 + hgrp * hb + i) for i in heads]
    s_old = [s_scr[i] for i in heads]

    qn, kn, v = [], [], []
    for i in heads:
        q = q_ref[0, :, cols[i]]
        k = k_ref[0, :, cols[i]]
        qn.append((q * lax.rsqrt(jnp.sum(q * q, axis=-1, keepdims=True) + EPS)) * (dk ** -0.5))
        kn.append(k * lax.rsqrt(jnp.sum(k * k, axis=-1, keepdims=True) + EPS))
        v.append(v_ref[0, :, cols[i]])
    knb = [x.astype(BF16) for x in kn]
    kbeta = [kn[i] * beta[i] for i in heads]
    lm = [jnp.exp(jnp.where(tril, dec_col[i] - dec_row[i], -jnp.inf)) for i in heads]
    ed = [jnp.exp(dec_col[i]) for i in heads]
    kq = [_dot_nt(jnp.concatenate([kbeta[i].astype(BF16), qn[i].astype(BF16)], axis=0), knb[i])
          for i in heads]
    a = [jnp.where(row > col, kq[i][:L] * lm[i], 0.0) for i in heads]
    qk = [(kq[i][L:] * lm[i]).astype(BF16) for i in heads]
    t = _unit_lower_inverses(a, row, col)
    rhs = [jnp.concatenate([v[i] * beta[i], kbeta[i] * ed[i]], axis=-1).astype(BF16) for i in heads]
    sol = [_dot(t[i].astype(BF16), rhs[i]) for i in heads]
    ws_qs = [_dot(jnp.concatenate([sol[i][:, dk:].astype(BF16), (qn[i] * ed[i]).astype(BF16)], axis=0),
                  s_old[i].astype(BF16)) for i in heads]
    vnb = [(sol[i][:, :dk] - ws_qs[i][:L]).astype(BF16) for i in heads]
    o = [ws_qs[i][L:] + _dot(qk[i], vnb[i]) for i in heads]
    for i in heads:
        dlast = dec_col[i][L - 1:L, :]
        kdec = kn[i] * jnp.exp(dlast - dec_col[i])
        s_scr[i] = s_old[i] * jnp.exp(dlast) + _dot(kdec.T.astype(BF16), vnb[i])
    for i in heads:
        on = o[i] * lax.rsqrt(jnp.mean(o[i] * o[i], axis=-1, keepdims=True) + EPS)
        y_ref[0, :, cols[i]] = ((on * ng_ref[...]) * _silu(z_ref[0, :, cols[i]])).astype(y_ref.dtype)


def _gdn(qkv, z, gates, gbias, alog_row, norm_g, *, n_heads, dk, beta_col0, g_col0):
    bsz, seq, _ = qkv.shape
    nc = seq // CHUNK
    hb = min(GDN_HEADS_PER_STEP, n_heads)
    ng = n_heads // hb
    bw = hb * dk
    return pl.pallas_call(
        functools.partial(_gdn_kernel, hb=hb, dk=dk, beta_col0=beta_col0, g_col0=g_col0),
        grid=(bsz, ng, nc),
        in_specs=[pl.BlockSpec((1, CHUNK, bw), lambda b, h, c: (b, c, h)),
                  pl.BlockSpec((1, CHUNK, bw), lambda b, h, c: (b, c, ng + h)),
                  pl.BlockSpec((1, CHUNK, bw), lambda b, h, c: (b, c, 2 * ng + h)),
                  pl.BlockSpec((1, CHUNK, bw), lambda b, h, c: (b, c, h)),
                  pl.BlockSpec((1, CHUNK, LANES), lambda b, h, c: (b, c, 0)),
                  pl.BlockSpec((1, LANES), lambda b, h, c: (0, 0)),
                  pl.BlockSpec((1, LANES), lambda b, h, c: (0, 0)),
                  pl.BlockSpec((1, dk), lambda b, h, c: (0, 0))],
        out_specs=pl.BlockSpec((1, CHUNK, bw), lambda b, h, c: (b, c, h)),
        out_shape=jax.ShapeDtypeStruct((bsz, seq, n_heads * dk), BF16),
        scratch_shapes=[pltpu.VMEM((hb, dk, dk), F32)],
        compiler_params=_params("parallel", "parallel", "arbitrary"),
        name="gdn",
    )(qkv, qkv, qkv, z, gates, gbias, alog_row, norm_g.reshape(1, dk))


def _ssd_kernel(x_ref, b_ref, c_ref, z_ref, dt_ref, dtb_ref, alog_ref, dskip_ref, ng_ref, y_ref,
                st_scr, *, hg, hp):
    g = pl.program_id(1)
    c = pl.program_id(2)

    @pl.when(c == 0)
    def _():
        st_scr[...] = jnp.zeros_like(st_scr)

    L = CHUNK
    row, col = _tri_masks(L)
    tril = row >= col
    dt_all = _softplus(dt_ref[0] + dtb_ref[...])
    a_all = dt_all * (-jnp.exp(alog_ref[...]))
    cs_all = _dot_f32(tril.astype(F32), a_all)
    shift = (LANES - g * hg) % LANES
    dt_g = pltpu.roll(dt_all, shift, axis=1)
    cs_g = pltpu.roll(cs_all, shift, axis=1)
    cs_gt = cs_g.T

    bm = b_ref[0]
    cm = c_ref[0]
    bb, cmb = bm.astype(BF16), cm.astype(BF16)
    cb = _dot_nt(cmb, bb)
    x = x_ref[0]
    st_old = st_scr[...]
    y_inter = _dot(cmb, st_old.astype(BF16))
    cs_last = cs_g[L - 1:L, :]
    lane = lax.broadcasted_iota(jnp.int32, (L, LANES), 1)
    lo = lane < hp
    lane1 = lax.broadcasted_iota(jnp.int32, (1, LANES), 1)
    lo1 = lane1 < hp

    y_parts, xw_parts, decay_parts = [], [], []
    for p in range(hg // 2):
        j0, j1 = 2 * p, 2 * p + 1
        xp = x[:, p * LANES:(p + 1) * LANES]
        dt_p = jnp.where(lo, dt_g[:, j0:j0 + 1], dt_g[:, j1:j1 + 1])
        cs_p = jnp.where(lo, cs_g[:, j0:j0 + 1], cs_g[:, j1:j1 + 1])
        last_p = jnp.where(lo1, cs_last[:, j0:j0 + 1], cs_last[:, j1:j1 + 1])
        xdt = xp * dt_p
        y_p = y_inter[:, p * LANES:(p + 1) * LANES] * jnp.exp(cs_p)
        for j, keep in ((j0, lo), (j1, jnp.logical_not(lo))):
            lm = jnp.exp(jnp.where(tril, cs_g[:, j:j + 1] - cs_gt[j:j + 1, :], -jnp.inf))
            mj = (cb * lm).astype(BF16)
            y_p = y_p + _dot(mj, jnp.where(keep, xdt, 0.0).astype(BF16))
        y_parts.append(y_p)
        xw_parts.append((xdt * jnp.exp(last_p - cs_p)).astype(BF16))
        decay_parts.append(jnp.exp(last_p))
    y = jnp.concatenate(y_parts, axis=1)
    xw = jnp.concatenate(xw_parts, axis=1)
    decay = jnp.concatenate(decay_parts, axis=1)
    st_scr[...] = st_old * decay + _dot(bm.T.astype(BF16), xw)

    y = (y + dskip_ref[...] * x) * _silu(z_ref[0])
    yn = y * lax.rsqrt(jnp.mean(y * y, axis=-1, keepdims=True) + EPS)
    y_ref[0] = (yn * ng_ref[...]).astype(y_ref.dtype)


def _ssd(xbc, zsrc, dtsrc, dt_bias, a_log, d_skip, norm_g, *, n_groups, hg, hp, ds):
    bsz, seq, _ = xbc.shape
    nc = seq // CHUNK
    gw = hg * hp
    d_inner = n_groups * gw
    assert hp * 2 == LANES and hg % 2 == 0 and n_groups * hg == LANES and ds == LANES
    bo = d_inner // ds
    dskip_row = jnp.repeat(d_skip, hp).reshape(1, d_inner)
    return pl.pallas_call(
        functools.partial(_ssd_kernel, hg=hg, hp=hp),
        grid=(bsz, n_groups, nc),
        in_specs=[pl.BlockSpec((1, CHUNK, gw), lambda b, g, c: (b, c, g)),
                  pl.BlockSpec((1, CHUNK, ds), lambda b, g, c: (b, c, bo + g)),
                  pl.BlockSpec((1, CHUNK, ds), lambda b, g, c: (b, c, bo + n_groups + g)),
                  pl.BlockSpec((1, CHUNK, gw), lambda b, g, c: (b, c, g)),
                  pl.BlockSpec((1, CHUNK, LANES), lambda b, g, c: (b, c, 0)),
                  pl.BlockSpec((1, LANES), lambda b, g, c: (0, 0)),
                  pl.BlockSpec((1, LANES), lambda b, g, c: (0, 0)),
                  pl.BlockSpec((1, gw), lambda b, g, c: (0, g)),
                  pl.BlockSpec((1, gw), lambda b, g, c: (0, g))],
        out_specs=pl.BlockSpec((1, CHUNK, gw), lambda b, g, c: (b, c, g)),
        out_shape=jax.ShapeDtypeStruct((bsz, seq, d_inner), BF16),
        scratch_shapes=[pltpu.VMEM((ds, gw), F32)],
        compiler_params=_params("parallel", "parallel", "arbitrary"),
        name="ssd",
    )(xbc, xbc, xbc, zsrc, dtsrc, dt_bias.reshape(1, LANES), a_log.reshape(1, LANES),
      dskip_row, norm_g.reshape(1, d_inner))


def _pad_row(parts, width=LANES):
    row = jnp.concatenate([p.astype(F32).reshape(-1) for p in parts])
    return jnp.pad(row, (0, width - row.shape[0])).reshape(1, width)


def _prep_weights(ev_w_in, ev_w_out, m2_w_in, m2_w_out, ffn_w_up, ffn_w_down, even_dims):
    ml_heads, ml_dk, ml_dv, gdn_heads, gdn_dk = even_dims
    c_gate = 2 * ml_heads * ml_dk + 2 * ml_heads * ml_dv
    c_qkv = c_gate + 2 * ml_heads
    c_gb = c_qkv + 4 * gdn_heads * gdn_dk
    n_small = 2 * ml_heads + 2 * gdn_heads
    small = jnp.concatenate([ev_w_in[:, :, c_gate:c_qkv], ev_w_in[:, :, c_gb:]], axis=2)
    small = jnp.pad(small, ((0, 0), (0, 0), (0, LANES - n_small)))
    return dict(ev_a=ev_w_in[:, :, :c_gate].astype(BF16), ev_b=ev_w_in[:, :, c_qkv:c_gb].astype(BF16),
                ev_s=small.astype(BF16), ev_out=ev_w_out.astype(BF16),
                m2_in=m2_w_in.astype(BF16), m2_out=m2_w_out.astype(BF16),
                up=ffn_w_up.astype(BF16), down=ffn_w_down.astype(BF16))


def _even_mixer(h2, u, bsz, seq, e, wts, ml_igate_b, ml_fgate_b, ml_norm_g, gdn_conv_w, gdn_A_log,
                gdn_dt_bias, gdn_norm_g, even_dims):
    ml_heads, ml_dk, ml_dv, gdn_heads, gdn_dk = even_dims
    ml_qk, ml_v = ml_heads * ml_dk, ml_heads * ml_dv
    gdn_v = gdn_heads * gdn_dk
    proj = _matmul([u], wts["ev_a"], e, name="ev_proj").reshape(bsz, seq, 2 * ml_qk + 2 * ml_v)
    qkv = _matmul_conv(u, wts["ev_b"], e, gdn_conv_w, None, seq=seq, col0=0, n=3 * gdn_v,
                       name="ev_qkv_conv").reshape(bsz, seq, 3 * gdn_v)
    z = _matmul([u], wts["ev_b"], e, col0=3 * gdn_v, n=gdn_v, name="ev_z").reshape(bsz, seq, gdn_v)
    gates = _matmul([u], wts["ev_s"], e, name="ev_gates").reshape(bsz, seq, LANES)
    gbias = _pad_row([ml_igate_b, ml_fgate_b, jnp.zeros((gdn_heads,), F32), gdn_dt_bias])
    alog_row = _pad_row([jnp.zeros((2 * ml_heads + gdn_heads,), F32), gdn_A_log])

    ya = _mlstm(proj, gates, gbias, ml_norm_g, n_heads=ml_heads, dk=ml_dk, dv=ml_dv,
                q_col0=0, k_col0=ml_qk, v_col0=2 * ml_qk, o_col0=2 * ml_qk + ml_v)
    yb = _gdn(qkv, z, gates, gbias, alog_row, gdn_norm_g, n_heads=gdn_heads, dk=gdn_dk,
              beta_col0=2 * ml_heads, g_col0=2 * ml_heads + gdn_heads)
    m = bsz * seq
    return _matmul([ya.reshape(m, ml_v), yb.reshape(m, gdn_v)], wts["ev_out"], e, res=h2,
                   row0s=(0, ml_v), name="ev_out")


def _mamba2_mixer(h2, u, bsz, seq, o, wts, conv_w, conv_b, dt_bias, a_log, d_skip, norm_g, dims):
    d_inner, n_heads, n_groups, ds = dims
    hp = d_inner // n_heads
    hg = n_heads // n_groups
    xbc_w = d_inner + 2 * n_groups * ds
    z = _matmul([u], wts["m2_in"], o, col0=0, n=d_inner, name="m2_z").reshape(bsz, seq, d_inner)
    xbc = _matmul_conv(u, wts["m2_in"], o, conv_w, conv_b, seq=seq, col0=d_inner, n=xbc_w,
                       name="m2_xbc_conv").reshape(bsz, seq, xbc_w)
    dt = _matmul([u], wts["m2_in"], o, col0=d_inner + xbc_w, n=n_heads, tn=LANES,
                 name="m2_dt").reshape(bsz, seq, n_heads)
    y = _ssd(xbc, z, dt, dt_bias, a_log, d_skip, norm_g, n_groups=n_groups, hg=hg, hp=hp, ds=ds)
    return _matmul([y.reshape(bsz * seq, d_inner)], wts["m2_out"], o, res=h2, name="m2_out")


def _conv_ffn(h2, u, seq, layer, wts, conv_w, conv_b):
    ffn = conv_w.shape[1]
    act = _matmul_conv(u, wts["up"], layer, conv_w, conv_b, seq=seq, col0=0, n=ffn, up_col0=ffn,
                       out_dtype=BF16, tn=512, name="ffn_up_conv")
    return _matmul([act], wts["down"], layer, res=h2, name="ffn_down")


def _trunk(x, norm_mix_g, norm_ffn_g, ev_w_in, ml_igate_b, ml_fgate_b, ml_norm_g, gdn_conv_w,
           gdn_A_log, gdn_dt_bias, gdn_norm_g, ev_w_out, m2_w_in, m2_conv_w, m2_conv_b, m2_dt_bias,
           m2_A_log, m2_D, m2_norm_g, m2_w_out, ffn_w_up, ffn_conv_w, ffn_conv_b, ffn_w_down,
           final_norm_g):
    bsz, seq, d = x.shape
    depth = norm_mix_g.shape[0]
    ml_heads = ml_igate_b.shape[1]
    ml_dv = ml_norm_g.shape[1] // ml_heads
    even_dims = (ml_heads, ml_dv // 2, ml_dv, gdn_A_log.shape[1], gdn_norm_g.shape[1])
    m2_inner = m2_norm_g.shape[1]
    m2_ds = (m2_conv_w.shape[2] - m2_inner) // (2 * M2_GROUPS)
    m2_dims = (m2_inner, m2_dt_bias.shape[1], M2_GROUPS, m2_ds)
    wts = _prep_weights(ev_w_in, ev_w_out, m2_w_in, m2_w_out, ffn_w_up, ffn_w_down, even_dims)

    h2 = x.reshape(bsz * seq, d)
    for layer in range(depth):
        u = _rmsnorm(h2, norm_mix_g[layer], BF16)
        if layer % 2 == 0:
            e = layer // 2
            h2 = _even_mixer(h2, u, bsz, seq, e, wts, ml_igate_b[e], ml_fgate_b[e], ml_norm_g[e],
                             gdn_conv_w[e], gdn_A_log[e], gdn_dt_bias[e], gdn_norm_g[e], even_dims)
        else:
            o = layer // 2
            h2 = _mamba2_mixer(h2, u, bsz, seq, o, wts, m2_conv_w[o], m2_conv_b[o], m2_dt_bias[o],
                               m2_A_log[o], m2_D[o], m2_norm_g[o], m2_dims)
        u = _rmsnorm(h2, norm_ffn_g[layer], BF16)
        h2 = _conv_ffn(h2, u, seq, layer, wts, ffn_conv_w[layer], ffn_conv_b[layer])
    return _rmsnorm(h2, final_norm_g, x.dtype).reshape(bsz, seq, d)


def kernel(x, norm_mix_g, norm_ffn_g, ev_w_in, ml_igate_b, ml_fgate_b, ml_norm_g, gdn_conv_w, gdn_A_log, gdn_dt_bias, gdn_norm_g, ev_w_out, m2_w_in, m2_conv_w, m2_conv_b, m2_dt_bias, m2_A_log, m2_D, m2_norm_g, m2_w_out, ffn_w_up, ffn_conv_w, ffn_conv_b, ffn_w_down, final_norm_g):
    return _trunk(x, norm_mix_g, norm_ffn_g, ev_w_in, ml_igate_b, ml_fgate_b, ml_norm_g, gdn_conv_w,
                  gdn_A_log, gdn_dt_bias, gdn_norm_g, ev_w_out, m2_w_in, m2_conv_w, m2_conv_b,
                  m2_dt_bias, m2_A_log, m2_D, m2_norm_g, m2_w_out, ffn_w_up, ffn_conv_w, ffn_conv_b,
                  ffn_w_down, final_norm_g)
```

```python
import functools

import jax
import jax.numpy as jnp
from jax import lax
from jax.experimental import pallas as pl
from jax.experimental.pallas import tpu as pltpu

CHUNK = 128
EPS = 1e-6
GATE_CAP = 15.0
LANES = 128
SUBLANES = 8
VMEM_LIMIT_BYTES = 58 * 1024 * 1024
M2_GROUPS = 8
GDN_HEADS_PER_STEP = 8
CONV_ROW_SUBTILE = 256
MM_ROW_SUBTILE = 256

F32 = jnp.float32
BF16 = jnp.bfloat16
HIGHEST = lax.Precision.HIGHEST


def _params(*sem):
    return pltpu.CompilerParams(dimension_semantics=sem, vmem_limit_bytes=VMEM_LIMIT_BYTES)


def _softplus(x):
    return jnp.maximum(x, 0.0) + jnp.log1p(jnp.exp(-jnp.abs(x)))


def _log_sigmoid(x):
    return -_softplus(-x)


def _sigmoid(x):
    return 1.0 / (1.0 + jnp.exp(-x))


def _silu(x):
    return x * _sigmoid(x)


def _dot(a, b):
    return jnp.dot(a, b, preferred_element_type=F32)


def _dot_nt(a, b):
    return lax.dot_general(a, b, (((1,), (1,)), ((), ())), preferred_element_type=F32)


def _dot_f32(a, b):
    return jnp.dot(a, b, preferred_element_type=F32, precision=HIGHEST)


def _tri_masks(n):
    row = lax.broadcasted_iota(jnp.int32, (n, n), 0)
    col = lax.broadcasted_iota(jnp.int32, (n, n), 1)
    return row, col


def _pick_col(tile, idx):
    lane = lax.broadcasted_iota(jnp.int32, tile.shape, 1)
    return jnp.sum(jnp.where(lane == idx, tile, 0.0), axis=1, keepdims=True)


def _pick_row(tile, idx):
    sub = lax.broadcasted_iota(jnp.int32, tile.shape, 0)
    return jnp.sum(jnp.where(sub == idx, tile, 0.0), axis=0, keepdims=True)


def _rmsnorm_kernel(x_ref, g_ref, o_ref):
    x = x_ref[...]
    ms = jnp.mean(x * x, axis=-1, keepdims=True)
    o_ref[...] = ((x * lax.rsqrt(ms + EPS)) * g_ref[...]).astype(o_ref.dtype)


def _rmsnorm(x, g, out_dtype, tm=256):
    m, d = x.shape
    tm = min(tm, m)
    return pl.pallas_call(
        _rmsnorm_kernel,
        grid=(m // tm,),
        in_specs=[pl.BlockSpec((tm, d), lambda i: (i, 0)),
                  pl.BlockSpec((1, d), lambda i: (0, 0))],
        out_specs=pl.BlockSpec((tm, d), lambda i: (i, 0)),
        out_shape=jax.ShapeDtypeStruct((m, d), out_dtype),
        compiler_params=_params("parallel"),
        name="rmsnorm",
    )(x, g.reshape(1, d))


def _mm_kernel(*refs, n_in, has_res, nk, rs):
    o_ref = refs[-1]
    res_ref = refs[2 * n_in] if has_res else None

    def sweep(first):
        for r in range(o_ref.shape[0] // rs):
            rows = slice(r * rs, (r + 1) * rs)
            acc = _dot(refs[0][rows, :], refs[1][...])
            for t in range(1, n_in):
                acc = acc + _dot(refs[2 * t][rows, :], refs[2 * t + 1][...])
            if first:
                o_ref[rows, :] = (acc + res_ref[rows, :]) if has_res else acc
            else:
                o_ref[rows, :] += acc

    if nk == 1:
        sweep(True)
    else:
        k = pl.program_id(2)
        pl.when(k == 0)(functools.partial(sweep, True))
        pl.when(k > 0)(functools.partial(sweep, False))


def _matmul(xs, w, layer, res=None, *, row0s=(0,), col0=0, n=None, tm=1024, tn=1024, tk=4096, name):
    m, kdim = xs[0].shape
    n = w.shape[2] - col0 if n is None else n
    tm, tn, tk = min(tm, m), min(tn, n), min(tk, kdim)
    nk = kdim // tk
    assert m % tm == 0 and n % tn == 0 and kdim % tk == 0 and col0 % tn == 0
    coff = col0 // tn
    in_specs, args = [], []
    for x, row0 in zip(xs, row0s):
        assert x.shape == (m, kdim) and row0 % tk == 0
        roff = row0 // tk
        in_specs += [pl.BlockSpec((tm, tk), lambda i, j, k: (i, k)),
                     pl.BlockSpec((None, tk, tn), lambda i, j, k, roff=roff: (layer, k + roff, j + coff))]
        args += [x, w]
    if res is not None:
        in_specs.append(pl.BlockSpec((tm, tn), lambda i, j, k: (i, j)))
        args.append(res)
    rs = min(MM_ROW_SUBTILE, tm) if (res is not None or nk > 1) else tm
    assert tm % rs == 0
    return pl.pallas_call(
        functools.partial(_mm_kernel, n_in=len(xs), has_res=res is not None, nk=nk, rs=rs),
        grid=(m // tm, n // tn, nk),
        in_specs=in_specs,
        out_specs=pl.BlockSpec((tm, tn), lambda i, j, k: (i, j)),
        out_shape=jax.ShapeDtypeStruct((m, n), F32),
        compiler_params=_params("parallel", "parallel", "arbitrary"),
        name=name,
    )(*args)


def _mm_conv_kernel(*refs, kw, has_bias, gated, tiles_per_seq, rs):
    x_ref, w_ref = refs[0], refs[1]
    pos = 2
    wu_ref = None
    if gated:
        wu_ref = refs[pos]
        pos += 1
    cw_ref = refs[pos]
    pos += 1
    cb_ref = None
    if has_bias:
        cb_ref = refs[pos]
        pos += 1
    o_ref, halo_ref = refs[pos], refs[pos + 1]
    acc_bufs = refs[pos + 2:pos + 4]
    up_bufs = refs[pos + 4:pos + 6] if gated else None
    i = pl.program_id(1)

    @pl.when(i % tiles_per_seq == 0)
    def _():
        halo_ref[...] = jnp.zeros_like(halo_ref)

    slot0 = jnp.minimum(i, 0)

    def project(r):
        xs = x_ref[r * rs:(r + 1) * rs, :]
        acc_bufs[r % 2][slot0] = _dot(xs, w_ref[...])
        if gated:
            up_bufs[r % 2][slot0] = _dot(xs, wu_ref[...])

    n_sub = x_ref.shape[0] // rs
    prev = halo_ref[...]
    project(0)
    for r in range(n_sub):
        if r + 1 < n_sub:
            project(r + 1)
        acc = acc_bufs[r % 2][slot0]
        y = cw_ref[kw - 1:kw, :] * acc
        for s in range(1, kw):
            y = y + cw_ref[kw - 1 - s:kw - s, :] * pltpu.roll(acc, s, axis=0)
        head = jnp.concatenate([prev, acc[:SUBLANES, :]], axis=0)
        yh = None
        for k in range(kw):
            off = SUBLANES - (kw - 1) + k
            term = cw_ref[k:k + 1, :] * head[off:off + SUBLANES, :]
            yh = term if yh is None else yh + term
        y = jnp.concatenate([yh, y[SUBLANES:, :]], axis=0)
        if has_bias:
            y = y + cb_ref[...]
        prev = acc[rs - SUBLANES:, :]
        act = _silu(y)
        if gated:
            act = act * up_bufs[r % 2][slot0]
        o_ref[r * rs:(r + 1) * rs, :] = act.astype(o_ref.dtype)
    halo_ref[...] = prev


def _matmul_conv(x, w, layer, conv_w, conv_b, *, seq, col0=0, n, up_col0=None, out_dtype=F32,
                 tm=1024, tn=1024, name):
    m, kdim = x.shape
    kw = conv_w.shape[0]
    tm, tn = min(tm, seq), min(tn, n)
    rs = min(CONV_ROW_SUBTILE, tm)
    assert seq % tm == 0 and n % tn == 0 and col0 % tn == 0 and tm % rs == 0 and kw - 1 <= SUBLANES
    coff = col0 // tn
    in_specs = [pl.BlockSpec((tm, kdim), lambda j, i: (i, 0)),
                pl.BlockSpec((None, kdim, tn), lambda j, i: (layer, 0, j + coff))]
    args = [x, w]
    gated = up_col0 is not None
    if gated:
        assert up_col0 % tn == 0
        uoff = up_col0 // tn
        in_specs.append(pl.BlockSpec((None, kdim, tn), lambda j, i: (layer, 0, j + uoff)))
        args.append(w)
    in_specs.append(pl.BlockSpec((kw, tn), lambda j, i: (0, j)))
    args.append(conv_w)
    if conv_b is not None:
        in_specs.append(pl.BlockSpec((1, tn), lambda j, i: (0, j)))
        args.append(conv_b.reshape(1, n))
    return pl.pallas_call(
        functools.partial(_mm_conv_kernel, kw=kw, has_bias=conv_b is not None, gated=gated,
                          tiles_per_seq=seq // tm, rs=rs),
        grid=(n // tn, m // tm),
        in_specs=in_specs,
        out_specs=pl.BlockSpec((tm, tn), lambda j, i: (i, j)),
        out_shape=jax.ShapeDtypeStruct((m, n), out_dtype),
        scratch_shapes=([pltpu.VMEM((SUBLANES, tn), F32)]
                        + [pltpu.VMEM((1, rs, tn), F32)] * (4 if gated else 2)),
        compiler_params=_params("arbitrary", "arbitrary"),
        name=name,
    )(*args)


def _even_gates_kernel(pre_ref, gbias_ref, alog_ref, col_ref, row_ref, *, n_ml, n_gdn):
    L = CHUNK
    row, col = _tri_masks(L)
    pre = pre_ref[0]
    lane = lax.broadcasted_iota(jnp.int32, pre.shape, 1)
    biased = pre + gbias_ref[...]
    capped = GATE_CAP * jnp.tanh(biased / GATE_CAP)
    log_f = _log_sigmoid(capped)
    beta = _sigmoid(pre)
    log_decay = -jnp.exp(alog_ref[...]) * _softplus(biased)
    is_f = (lane >= n_ml) & (lane < 2 * n_ml)
    is_beta = (lane >= 2 * n_ml) & (lane < 2 * n_ml + n_gdn)
    is_decay = (lane >= 2 * n_ml + n_gdn) & (lane < 2 * n_ml + 2 * n_gdn)
    cum = _dot_f32((row >= col).astype(F32), jnp.where(is_f, log_f, jnp.where(is_decay, log_decay, 0.0)))
    out = jnp.where(lane < n_ml, capped, jnp.where(is_beta, beta, cum))
    col_ref[0] = out
    row_ref[0, 0] = out.T


def _even_gates(gates, gbias, alog_row, *, n_ml, n_gdn):
    bsz, seq, _ = gates.shape
    nc = seq // CHUNK
    return pl.pallas_call(
        functools.partial(_even_gates_kernel, n_ml=n_ml, n_gdn=n_gdn),
        grid=(bsz, nc),
        in_specs=[pl.BlockSpec((1, CHUNK, LANES), lambda b, c: (b, c, 0)),
                  pl.BlockSpec((1, LANES), lambda b, c: (0, 0)),
                  pl.BlockSpec((1, LANES), lambda b, c: (0, 0))],
        out_specs=[pl.BlockSpec((1, CHUNK, LANES), lambda b, c: (b, c, 0)),
                   pl.BlockSpec((1, 1, LANES, CHUNK), lambda b, c: (b, c, 0, 0))],
        out_shape=[jax.ShapeDtypeStruct((bsz, seq, LANES), F32),
                   jax.ShapeDtypeStruct((bsz, nc, LANES, CHUNK), F32)],
        compiler_params=_params("parallel", "parallel"),
        name="even_gates",
    )(gates, gbias, alog_row)


def _m2_gates_kernel(dt_ref, dtb_ref, alog_ref, dtg_ref, csg_ref, cst_ref, *, n_groups, hg):
    L = CHUNK
    row, col = _tri_masks(L)
    dt = _softplus(dt_ref[0] + dtb_ref[...])
    cs = _dot_f32((row >= col).astype(F32), dt * (-jnp.exp(alog_ref[...])))
    for g in range(n_groups):
        shift = (LANES - g * hg) % LANES
        dtg_ref[0, :, g * LANES:(g + 1) * LANES] = dt if shift == 0 else pltpu.roll(dt, shift, axis=1)
        csg_ref[0, :, g * LANES:(g + 1) * LANES] = cs if shift == 0 else pltpu.roll(cs, shift, axis=1)
    cst_ref[0, 0] = cs.T


def _m2_gates(dt_pre, dt_bias, a_log, *, n_groups, hg):
    bsz, seq, _ = dt_pre.shape
    nc = seq // CHUNK
    return pl.pallas_call(
        functools.partial(_m2_gates_kernel, n_groups=n_groups, hg=hg),
        grid=(bsz, nc),
        in_specs=[pl.BlockSpec((1, CHUNK, LANES), lambda b, c: (b, c, 0)),
                  pl.BlockSpec((1, LANES), lambda b, c: (0, 0)),
                  pl.BlockSpec((1, LANES), lambda b, c: (0, 0))],
        out_specs=[pl.BlockSpec((1, CHUNK, n_groups * LANES), lambda b, c: (b, c, 0)),
                   pl.BlockSpec((1, CHUNK, n_groups * LANES), lambda b, c: (b, c, 0)),
                   pl.BlockSpec((1, 1, LANES, CHUNK), lambda b, c: (b, c, 0, 0))],
        out_shape=[jax.ShapeDtypeStruct((bsz, seq, n_groups * LANES), F32),
                   jax.ShapeDtypeStruct((bsz, seq, n_groups * LANES), F32),
                   jax.ShapeDtypeStruct((bsz, nc, LANES, CHUNK), F32)],
        compiler_params=_params("parallel", "parallel"),
        name="m2_gates",
    )(dt_pre, dt_bias.reshape(1, LANES), a_log.reshape(1, LANES))


def _mlstm_kernel(q_ref, k_ref, v_ref, og_ref, gcol_ref, grow_ref, ng_ref, y_ref,
                  c_scr, n_scr, m_scr, *, n_heads, dk, dv):
    c = pl.program_id(1)

    @pl.when(c == 0)
    def _():
        c_scr[...] = jnp.zeros_like(c_scr)
        n_scr[...] = jnp.zeros_like(n_scr)
        m_scr[...] = jnp.zeros_like(m_scr)

    L = CHUNK
    row, col = _tri_masks(L)
    tril = row >= col
    gcol = gcol_ref[0]
    grow = grow_ref[0, 0]
    heads = range(n_heads)
    i_col = [gcol[:, h:h + 1] for h in heads]
    cb_col = [gcol[:, n_heads + h:n_heads + h + 1] for h in heads]
    i_row = [grow[h:h + 1, :] for h in heads]
    cb_row = [grow[n_heads + h:n_heads + h + 1, :] for h in heads]
    q = [q_ref[0, :, h * dk:(h + 1) * dk] for h in heads]
    k = [k_ref[0, :, h * dk:(h + 1) * dk] * (dk ** -0.5) for h in heads]
    qb = [x.astype(BF16) for x in q]
    kb = [x.astype(BF16) for x in k]
    vb = [v_ref[0, :, h * dv:(h + 1) * dv].astype(BF16) for h in heads]
    m_prev = [m_scr[h] for h in heads]
    c_old = [c_scr[h] for h in heads]
    n_old = [n_scr[h] for h in heads]

    d = [jnp.where(tril, cb_col[h] - cb_row[h] + i_row[h], -jnp.inf) for h in heads]
    inter = [cb_col[h] + m_prev[h] for h in heads]
    mt = [jnp.maximum(inter[h], jnp.max(d[h], axis=1, keepdims=True)) for h in heads]
    qk = [_dot_nt(qb[h], kb[h]) for h in heads]
    qc = [_dot(qb[h], c_old[h].astype(BF16)) for h in heads]
    sc = [qk[h] * jnp.exp(d[h] - mt[h]) for h in heads]
    g = [jnp.exp(inter[h] - mt[h]) for h in heads]
    num = [_dot(sc[h].astype(BF16), vb[h]) + g[h] * qc[h] for h in heads]
    den = [jnp.sum(sc[h], axis=1, keepdims=True) + g[h] * jnp.sum(q[h] * n_old[h], axis=1, keepdims=True)
           for h in heads]
    hh = [num[h] / jnp.maximum(jnp.abs(den[h]), jnp.exp(-mt[h])) for h in heads]

    for h in heads:
        bl = cb_col[h][L - 1:L, :]
        wl = bl - cb_col[h] + i_col[h]
        m_new = jnp.maximum(bl + m_prev[h], jnp.max(wl, axis=0, keepdims=True))
        kw = k[h] * jnp.exp(wl - m_new)
        gs = jnp.exp(bl + m_prev[h] - m_new)
        c_scr[h] = gs * c_old[h] + _dot(kw.T.astype(BF16), vb[h])
        n_scr[h] = gs * n_old[h] + jnp.sum(kw, axis=0, keepdims=True)
        m_scr[h] = m_new
    for h in heads:
        cols = slice(h * dv, (h + 1) * dv)
        hn = hh[h] * lax.rsqrt(jnp.mean(hh[h] * hh[h], axis=-1, keepdims=True) + EPS)
        y_ref[0, :, cols] = ((hn * ng_ref[:, cols]) * _sigmoid(og_ref[0, :, cols])).astype(y_ref.dtype)


def _mlstm(proj, gcol, grow, norm_g, *, n_heads, dk, dv):
    bsz, seq, _ = proj.shape
    nc = seq // CHUNK
    qk_w, v_w = n_heads * dk, n_heads * dv
    assert v_w == 2 * qk_w
    return pl.pallas_call(
        functools.partial(_mlstm_kernel, n_heads=n_heads, dk=dk, dv=dv),
        grid=(bsz, nc),
        in_specs=[pl.BlockSpec((1, CHUNK, qk_w), lambda b, c: (b, c, 0)),
                  pl.BlockSpec((1, CHUNK, qk_w), lambda b, c: (b, c, 1)),
                  pl.BlockSpec((1, CHUNK, v_w), lambda b, c: (b, c, 1)),
                  pl.BlockSpec((1, CHUNK, v_w), lambda b, c: (b, c, 2)),
                  pl.BlockSpec((1, CHUNK, LANES), lambda b, c: (b, c, 0)),
                  pl.BlockSpec((1, 1, LANES, CHUNK), lambda b, c: (b, c, 0, 0)),
                  pl.BlockSpec((1, v_w), lambda b, c: (0, 0))],
        out_specs=pl.BlockSpec((1, CHUNK, v_w), lambda b, c: (b, c, 0)),
        out_shape=jax.ShapeDtypeStruct((bsz, seq, v_w), BF16),
        scratch_shapes=[pltpu.VMEM((n_heads, dk, dv), F32), pltpu.VMEM((n_heads, 1, dk), F32),
                        pltpu.VMEM((n_heads, 1, 1), F32)],
        compiler_params=_params("parallel", "arbitrary"),
        name="mlstm",
    )(proj, proj, proj, proj, gcol, grow, norm_g.reshape(1, v_w))


def _split_bf16(x):
    hi = x.astype(BF16)
    lo = (x - hi.astype(F32)).astype(BF16)
    return hi, lo


def _dot_split(a, b):
    n = a.shape[0]
    ah, al = _split_bf16(a)
    bh, bl = _split_bf16(b)
    rb = jnp.concatenate([bh, bl], axis=1)
    r = _dot(jnp.concatenate([ah, al], axis=1), jnp.concatenate([rb, rb], axis=0))
    return r[:, :n] + r[:, n:]


def _unit_lower_inverses(mats, row, col):
    n = mats[0].shape[0]

    def same_block(size):
        return (row // size) == (col // size)

    base = min(16, n)
    eye = jnp.where(row == col, 1.0, 0.0)
    in_base = same_block(base)
    ps = [jnp.where(in_base, a, 0.0) for a in mats]
    ts = [eye - p for p in ps]
    size = 2
    while size < base:
        ps = [_dot_split(p, p) for p in ps]
        ts = [t + _dot_split(t, p) for t, p in zip(ts, ps)]
        size *= 2
    size = base
    while size < n:
        ring = same_block(2 * size) & jnp.logical_not(same_block(size))
        tos = [_dot_split(t, jnp.where(ring, a, 0.0)) for t, a in zip(ts, mats)]
        ts = [t - _dot_split(to, t) for t, to in zip(ts, tos)]
        size *= 2
    return ts


def _gdn_kernel(q_ref, k_ref, v_ref, z_ref, gcol_ref, grow_ref, ng_ref, y_ref, s_scr,
                *, hb, dk, beta_col0, g_col0):
    hgrp = pl.program_id(1)
    c = pl.program_id(2)

    @pl.when(c == 0)
    def _():
        s_scr[...] = jnp.zeros_like(s_scr)

    L = CHUNK
    row, col = _tri_masks(L)
    tril = row >= col
    gcol = gcol_ref[0]
    grow = grow_ref[0, 0]

    heads = range(hb)
    cols = [slice(i * dk, (i + 1) * dk) for i in heads]
    beta = [_pick_col(gcol, beta_col0 + hgrp * hb + i) for i in heads]
    dec_col = [_pick_col(gcol, g_col0 + hgrp * hb + i) for i in heads]
    dec_row = [_pick_row(grow, g_col0 + hgrp * hb + i) for i in heads]
    s_old = [s_scr[i] for i in heads]

    qn, kn, v = [], [], []
    for i in heads:
        q = q_ref[0, :, cols[i]]
        k = k_ref[0, :, cols[i]]
        qn.append((q * lax.rsqrt(jnp.sum(q * q, axis=-1, keepdims=True) + EPS)) * (dk ** -0.5))
        kn.append(k * lax.rsqrt(jnp.sum(k * k, axis=-1, keepdims=True) + EPS))
        v.append(v_ref[0, :, cols[i]])
    knb = [x.astype(BF16) for x in kn]
    kbeta = [kn[i] * beta[i] for i in heads]
    lm = [jnp.exp(jnp.where(tril, dec_col[i] - dec_row[i], -jnp.inf)) for i in heads]
    ed = [jnp.exp(dec_col[i]) for i in heads]
    kq = [_dot_nt(jnp.concatenate([kbeta[i].astype(BF16), qn[i].astype(BF16)], axis=0), knb[i])
          for i in heads]
    a = [jnp.where(row > col, kq[i][:L] * lm[i], 0.0) for i in heads]
    qk = [(kq[i][L:] * lm[i]).astype(BF16) for i in heads]
    t = _unit_lower_inverses(a, row, col)
    rhs = [jnp.concatenate([v[i] * beta[i], kbeta[i] * ed[i]], axis=-1).astype(BF16) for i in heads]
    sol = [_dot(t[i].astype(BF16), rhs[i]) for i in heads]
    ws_qs = [_dot(jnp.concatenate([sol[i][:, dk:].astype(BF16), (qn[i] * ed[i]).astype(BF16)], axis=0),
                  s_old[i].astype(BF16)) for i in heads]
    vnb = [(sol[i][:, :dk] - ws_qs[i][:L]).astype(BF16) for i in heads]
    o = [ws_qs[i][L:] + _dot(qk[i], vnb[i]) for i in heads]
    for i in heads:
        dlast = dec_col[i][L - 1:L, :]
        kdec = kn[i] * jnp.exp(dlast - dec_col[i])
        s_scr[i] = s_old[i] * jnp.exp(dlast) + _dot(kdec.T.astype(BF16), vnb[i])
    for i in heads:
        on = o[i] * lax.rsqrt(jnp.mean(o[i] * o[i], axis=-1, keepdims=True) + EPS)
        y_ref[0, :, cols[i]] = ((on * ng_ref[...]) * _silu(z_ref[0, :, cols[i]])).astype(y_ref.dtype)


def _gdn(qkv, z, gcol, grow, norm_g, *, n_heads, dk, beta_col0, g_col0):
    bsz, seq, _ = qkv.shape
    nc = seq // CHUNK
    hb = min(GDN_HEADS_PER_STEP, n_heads)
    ng = n_heads // hb
    bw = hb * dk
    return pl.pallas_call(
        functools.partial(_gdn_kernel, hb=hb, dk=dk, beta_col0=beta_col0, g_col0=g_col0),
        grid=(bsz, ng, nc),
        in_specs=[pl.BlockSpec((1, CHUNK, bw), lambda b, h, c: (b, c, h)),
                  pl.BlockSpec((1, CHUNK, bw), lambda b, h, c: (b, c, ng + h)),
                  pl.BlockSpec((1, CHUNK, bw), lambda b, h, c: (b, c, 2 * ng + h)),
                  pl.BlockSpec((1, CHUNK, bw), lambda b, h, c: (b, c, h)),
                  pl.BlockSpec((1, CHUNK, LANES), lambda b, h, c: (b, c, 0)),
                  pl.BlockSpec((1, 1, LANES, CHUNK), lambda b, h, c: (b, c, 0, 0)),
                  pl.BlockSpec((1, dk), lambda b, h, c: (0, 0))],
        out_specs=pl.BlockSpec((1, CHUNK, bw), lambda b, h, c: (b, c, h)),
        out_shape=jax.ShapeDtypeStruct((bsz, seq, n_heads * dk), BF16),
        scratch_shapes=[pltpu.VMEM((hb, dk, dk), F32)],
        compiler_params=_params("parallel", "parallel", "arbitrary"),
        name="gdn",
    )(qkv, qkv, qkv, z, gcol, grow, norm_g.reshape(1, dk))


def _ssd_kernel(x_ref, b_ref, c_ref, z_ref, dtg_ref, csg_ref, cst_ref, dskip_ref, ng_ref, y_ref,
                st_scr, *, hg, hp):
    c = pl.program_id(2)

    @pl.when(c == 0)
    def _():
        st_scr[...] = jnp.zeros_like(st_scr)

    L = CHUNK
    row, col = _tri_masks(L)
    tril = row >= col
    dt_g = dtg_ref[0]
    cs_g = csg_ref[0]
    cs_gt = cst_ref[0, 0]

    bm = b_ref[0]
    cm = c_ref[0]
    bb, cmb = bm.astype(BF16), cm.astype(BF16)
    cb = _dot_nt(cmb, bb)
    x = x_ref[0]
    st_old = st_scr[...]
    y_inter = _dot(cmb, st_old.astype(BF16))
    cs_last = cs_g[L - 1:L, :]
    lane = lax.broadcasted_iota(jnp.int32, (L, LANES), 1)
    lo = lane < hp
    lane1 = lax.broadcasted_iota(jnp.int32, (1, LANES), 1)
    lo1 = lane1 < hp

    pairs = range(hg // 2)
    xp = [x[:, p * LANES:(p + 1) * LANES] for p in pairs]
    dt_p = [jnp.where(lo, dt_g[:, 2 * p:2 * p + 1], dt_g[:, 2 * p + 1:2 * p + 2]) for p in pairs]
    cs_p = [jnp.where(lo, cs_g[:, 2 * p:2 * p + 1], cs_g[:, 2 * p + 1:2 * p + 2]) for p in pairs]
    last_p = [jnp.where(lo1, cs_last[:, 2 * p:2 * p + 1], cs_last[:, 2 * p + 1:2 * p + 2]) for p in pairs]
    xdt = [xp[p] * dt_p[p] for p in pairs]
    mats = [(cb * jnp.exp(jnp.where(tril, cs_g[:, j:j + 1] - cs_gt[j:j + 1, :], -jnp.inf))).astype(BF16)
            for j in range(hg)]
    stacked = [jnp.concatenate([jnp.where(lo, xdt[p], 0.0), jnp.where(lo, 0.0, xdt[p])],
                               axis=0).astype(BF16) for p in pairs]
    intra = [_dot(jnp.concatenate([mats[2 * p], mats[2 * p + 1]], axis=1), stacked[p]) for p in pairs]
    y = jnp.concatenate([y_inter[:, p * LANES:(p + 1) * LANES] * jnp.exp(cs_p[p]) + intra[p]
                         for p in pairs], axis=1)
    xw = jnp.concatenate([(xdt[p] * jnp.exp(last_p[p] - cs_p[p])).astype(BF16) for p in pairs], axis=1)
    decay = jnp.concatenate([jnp.exp(last_p[p]) for p in pairs], axis=1)
    st_scr[...] = st_old * decay + _dot(bm.T.astype(BF16), xw)

    y = (y + dskip_ref[...] * x) * _silu(z_ref[0])
    yn = y * lax.rsqrt(jnp.mean(y * y, axis=-1, keepdims=True) + EPS)
    y_ref[0] = (yn * ng_ref[...]).astype(y_ref.dtype)


def _ssd(xbc, zsrc, dtg, csg, cst, d_skip, norm_g, *, n_groups, hg, hp, ds):
    bsz, seq, _ = xbc.shape
    nc = seq // CHUNK
    gw = hg * hp
    d_inner = n_groups * gw
    assert hp * 2 == LANES and hg % 2 == 0 and n_groups * hg == LANES and ds == LANES
    assert hg % SUBLANES == 0
    bo = d_inner // ds
    dskip_row = jnp.repeat(d_skip, hp).reshape(1, d_inner)
    return pl.pallas_call(
        functools.partial(_ssd_kernel, hg=hg, hp=hp),
        grid=(bsz, n_groups, nc),
        in_specs=[pl.BlockSpec((1, CHUNK, gw), lambda b, g, c: (b, c, g)),
                  pl.BlockSpec((1, CHUNK, ds), lambda b, g, c: (b, c, bo + g)),
                  pl.BlockSpec((1, CHUNK, ds), lambda b, g, c: (b, c, bo + n_groups + g)),
                  pl.BlockSpec((1, CHUNK, gw), lambda b, g, c: (b, c, g)),
                  pl.BlockSpec((1, CHUNK, LANES), lambda b, g, c: (b, c, g)),
                  pl.BlockSpec((1, CHUNK, LANES), lambda b, g, c: (b, c, g)),
                  pl.BlockSpec((1, 1, hg, CHUNK), lambda b, g, c: (b, c, g, 0)),
                  pl.BlockSpec((1, gw), lambda b, g, c: (0, g)),
                  pl.BlockSpec((1, gw), lambda b, g, c: (0, g))],
        out_specs=pl.BlockSpec((1, CHUNK, gw), lambda b, g, c: (b, c, g)),
        out_shape=jax.ShapeDtypeStruct((bsz, seq, d_inner), BF16),
        scratch_shapes=[pltpu.VMEM((ds, gw), F32)],
        compiler_params=_params("parallel", "parallel", "arbitrary"),
        name="ssd",
    )(xbc, xbc, xbc, zsrc, dtg, csg, cst, dskip_row, norm_g.reshape(1, d_inner))


def _pad_row(parts, width=LANES):
    row = jnp.concatenate([p.astype(F32).reshape(-1) for p in parts])
    return jnp.pad(row, (0, width - row.shape[0])).reshape(1, width)


def _prep_weights(ev_w_in, ev_w_out, m2_w_in, m2_w_out, ffn_w_up, ffn_w_down, even_dims):
    ml_heads, ml_dk, ml_dv, gdn_heads, gdn_dk = even_dims
    c_gate = 2 * ml_heads * ml_dk + 2 * ml_heads * ml_dv
    c_qkv = c_gate + 2 * ml_heads
    c_gb = c_qkv + 4 * gdn_heads * gdn_dk
    n_small = 2 * ml_heads + 2 * gdn_heads
    small = jnp.concatenate([ev_w_in[:, :, c_gate:c_qkv], ev_w_in[:, :, c_gb:]], axis=2)
    small = jnp.pad(small, ((0, 0), (0, 0), (0, LANES - n_small)))
    return dict(ev_a=ev_w_in[:, :, :c_gate].astype(BF16), ev_b=ev_w_in[:, :, c_qkv:c_gb].astype(BF16),
                ev_s=small.astype(BF16), ev_out=ev_w_out.astype(BF16),
                m2_in=m2_w_in.astype(BF16), m2_out=m2_w_out.astype(BF16),
                up=ffn_w_up.astype(BF16), down=ffn_w_down.astype(BF16))


def _even_mixer(h2, u, bsz, seq, e, wts, ml_igate_b, ml_fgate_b, ml_norm_g, gdn_conv_w, gdn_A_log,
                gdn_dt_bias, gdn_norm_g, even_dims):
    ml_heads, ml_dk, ml_dv, gdn_heads, gdn_dk = even_dims
    ml_qk, ml_v = ml_heads * ml_dk, ml_heads * ml_dv
    gdn_v = gdn_heads * gdn_dk
    proj = _matmul([u], wts["ev_a"], e, name="ev_proj").reshape(bsz, seq, 2 * ml_qk + 2 * ml_v)
    qkv = _matmul_conv(u, wts["ev_b"], e, gdn_conv_w, None, seq=seq, col0=0, n=3 * gdn_v,
                       name="ev_qkv_conv").reshape(bsz, seq, 3 * gdn_v)
    z = _matmul([u], wts["ev_b"], e, col0=3 * gdn_v, n=gdn_v, name="ev_z").reshape(bsz, seq, gdn_v)
    gates = _matmul([u], wts["ev_s"], e, name="ev_gates").reshape(bsz, seq, LANES)
    gbias = _pad_row([ml_igate_b, ml_fgate_b, jnp.zeros((gdn_heads,), F32), gdn_dt_bias])
    alog_row = _pad_row([jnp.zeros((2 * ml_heads + gdn_heads,), F32), gdn_A_log])

    gcol, grow = _even_gates(gates, gbias, alog_row, n_ml=ml_heads, n_gdn=gdn_heads)
    ya = _mlstm(proj, gcol, grow, ml_norm_g, n_heads=ml_heads, dk=ml_dk, dv=ml_dv)
    yb = _gdn(qkv, z, gcol, grow, gdn_norm_g, n_heads=gdn_heads, dk=gdn_dk,
              beta_col0=2 * ml_heads, g_col0=2 * ml_heads + gdn_heads)
    m = bsz * seq
    return _matmul([ya.reshape(m, ml_v), yb.reshape(m, gdn_v)], wts["ev_out"], e, res=h2,
                   row0s=(0, ml_v), name="ev_out")


def _mamba2_mixer(h2, u, bsz, seq, o, wts, conv_w, conv_b, dt_bias, a_log, d_skip, norm_g, dims):
    d_inner, n_heads, n_groups, ds = dims
    hp = d_inner // n_heads
    hg = n_heads // n_groups
    xbc_w = d_inner + 2 * n_groups * ds
    z = _matmul([u], wts["m2_in"], o, col0=0, n=d_inner, name="m2_z").reshape(bsz, seq, d_inner)
    xbc = _matmul_conv(u, wts["m2_in"], o, conv_w, conv_b, seq=seq, col0=d_inner, n=xbc_w,
                       name="m2_xbc_conv").reshape(bsz, seq, xbc_w)
    dt = _matmul([u], wts["m2_in"], o, col0=d_inner + xbc_w, n=n_heads, tn=LANES,
                 name="m2_dt").reshape(bsz, seq, n_heads)
    dtg, csg, cst = _m2_gates(dt, dt_bias, a_log, n_groups=n_groups, hg=hg)
    y = _ssd(xbc, z, dtg, csg, cst, d_skip, norm_g, n_groups=n_groups, hg=hg, hp=hp, ds=ds)
    return _matmul([y.reshape(bsz * seq, d_inner)], wts["m2_out"], o, res=h2, name="m2_out")


def _conv_ffn(h2, u, seq, layer, wts, conv_w, conv_b):
    ffn = conv_w.shape[1]
    act = _matmul_conv(u, wts["up"], layer, conv_w, conv_b, seq=seq, col0=0, n=ffn, up_col0=ffn,
                       out_dtype=BF16, tn=512, name="ffn_up_conv")
    return _matmul([act], wts["down"], layer, res=h2, name="ffn_down")


def _trunk(x, norm_mix_g, norm_ffn_g, ev_w_in, ml_igate_b, ml_fgate_b, ml_norm_g, gdn_conv_w,
           gdn_A_log, gdn_dt_bias, gdn_norm_g, ev_w_out, m2_w_in, m2_conv_w, m2_conv_b, m2_dt_bias,
           m2_A_log, m2_D, m2_norm_g, m2_w_out, ffn_w_up, ffn_conv_w, ffn_conv_b, ffn_w_down,
           final_norm_g):
    bsz, seq, d = x.shape
    depth = norm_mix_g.shape[0]
    ml_heads = ml_igate_b.shape[1]
    ml_dv = ml_norm_g.shape[1] // ml_heads
    even_dims = (ml_heads, ml_dv // 2, ml_dv, gdn_A_log.shape[1], gdn_norm_g.shape[1])
    m2_inner = m2_norm_g.shape[1]
    m2_ds = (m2_conv_w.shape[2] - m2_inner) // (2 * M2_GROUPS)
    m2_dims = (m2_inner, m2_dt_bias.shape[1], M2_GROUPS, m2_ds)
    wts = _prep_weights(ev_w_in, ev_w_out, m2_w_in, m2_w_out, ffn_w_up, ffn_w_down, even_dims)

    h2 = x.reshape(bsz * seq, d)
    for layer in range(depth):
        u = _rmsnorm(h2, norm_mix_g[layer], BF16)
        if layer % 2 == 0:
            e = layer // 2
            h2 = _even_mixer(h2, u, bsz, seq, e, wts, ml_igate_b[e], ml_fgate_b[e], ml_norm_g[e],
                             gdn_conv_w[e], gdn_A_log[e], gdn_dt_bias[e], gdn_norm_g[e], even_dims)
        else:
            o = layer // 2
            h2 = _mamba2_mixer(h2, u, bsz, seq, o, wts, m2_conv_w[o], m2_conv_b[o], m2_dt_bias[o],
                               m2_A_log[o], m2_D[o], m2_norm_g[o], m2_dims)
        u = _rmsnorm(h2, norm_ffn_g[layer], BF16)
        h2 = _conv_ffn(h2, u, seq, layer, wts, ffn_conv_w[layer], ffn_conv_b[layer])
    return _rmsnorm(h2, final_norm_g, x.dtype).reshape(bsz, seq, d)


def kernel(x, norm_mix_g, norm_ffn_g, ev_w_in, ml_igate_b, ml_fgate_b, ml_norm_g, gdn_conv_w, gdn_A_log, gdn_dt_bias, gdn_norm_g, ev_w_out, m2_w_in, m2_conv_w, m2_conv_b, m2_dt_bias, m2_A_log, m2_D, m2_norm_g, m2_w_out, ffn_w_up, ffn_conv_w, ffn_conv_b, ffn_w_down, final_norm_g):
    return _trunk(x, norm_mix_g, norm_ffn_g, ev_w_in, ml_igate_b, ml_fgate_b, ml_norm_g, gdn_conv_w,
                  gdn_A_log, gdn_dt_bias, gdn_norm_g, ev_w_out, m2_w_in, m2_conv_w, m2_conv_b,
                  m2_dt_bias, m2_A_log, m2_D, m2_norm_g, m2_w_out, ffn_w_up, ffn_conv_w, ffn_conv_b,
                  ffn_w_down, final_norm_g)
```
